```python
import math
import jax
import jax.numpy as jnp
from jax import lax
import numpy as np

D_MODEL = 1024
BATCH = 4
SEQ = 4096
DEPTH = 2

GRID_W = 64
CTX_LEN = 256
D_MIX = D_MODEL
EPS = 1e-6

MLA_HEADS = 6
MLA_NOPE = 64
MLA_ROPE = 32
MLA_QK = MLA_NOPE + MLA_ROPE
MLA_V = 64
MLA_Q_LORA = 256
MLA_KV_LORA = 128
ROPE_THETA = 10000.0
ROPE_FREQS = MLA_ROPE // 4
Q_BLOCK = 128

SSM_HEADS = 6
SSM_HEAD_DIM = 64
SSM_INNER = SSM_HEADS * SSM_HEAD_DIM
SSM_GROUPS = 2
SSM_STATE = 64
SSM_CONV = 3
SSM_CHUNK = 128
SSM_XBC = SSM_INNER + 2 * SSM_GROUPS * SSM_STATE

HY_CH = D_MIX - MLA_HEADS * MLA_V - SSM_INNER
HY_ORDER = 2
HY_CONV = 3
HY_BANDS = 16
HY_EMB = 1 + 2 * HY_BANDS
HY_HIDDEN = 64
HY_DECAY_PCT_SHORT = 0.3
HY_DECAY_PCT_LONG = 1.5
HY_DECAY_TARGET = 1e-2

D_FF = 4 * D_MODEL

IN_SPLITS = (MLA_Q_LORA, MLA_KV_LORA, MLA_ROPE, SSM_INNER, SSM_XBC, 2 * SSM_HEADS, (HY_ORDER + 1) * HY_CH)
IN_COLS = sum(IN_SPLITS)
IN_OFFSETS = tuple(int(v) for v in np.cumsum(IN_SPLITS)[:-1])

kernel_name = 'hybrid_mla_ssd_hyena_prefix_block'


def rms_norm(x, g):
    xf = x.astype(jnp.float32)
    y = xf * lax.rsqrt(jnp.mean(xf * xf, axis=-1, keepdims=True) + EPS)
    return (y * g.astype(jnp.float32)).astype(x.dtype)


def modulate(h, shift, scale):
    return h * (1 + scale) + shift


def dw_conv(u, w, b):
    pad = w.shape[0] // 2
    y = lax.conv_general_dilated(u, w[:, None, :].astype(u.dtype), window_strides=(1,),
                                 padding=[(pad, pad)], dimension_numbers=('NWC', 'WIO', 'NWC'),
                                 feature_group_count=u.shape[-1])
    return y + b.astype(u.dtype)


def axial_rope_tables(length):
    rows = length // GRID_W
    row = jnp.repeat(jnp.arange(rows), GRID_W).astype(jnp.float32)
    col = (jnp.arange(rows * GRID_W) % GRID_W).astype(jnp.float32)
    inv = ROPE_THETA ** (-jnp.arange(ROPE_FREQS, dtype=jnp.float32) / ROPE_FREQS)
    ang = jnp.stack([row[:, None] * inv, col[:, None] * inv], axis=1)
    return jnp.cos(ang), jnp.sin(ang)


def apply_axial_rope(x, cos, sin):
    xr = x.astype(jnp.float32).reshape(x.shape[:-1] + (2, 2, ROPE_FREQS))
    x1, x2 = xr[..., 0, :], xr[..., 1, :]
    cs, sn = cos[None, :, None], sin[None, :, None]
    out = jnp.stack([x1 * cs - x2 * sn, x2 * cs + x1 * sn], axis=-2)
    return out.reshape(x.shape).astype(x.dtype)


def mla_query(cq, g_cq, w_uq, g_qh, rope):
    b, l, _ = cq.shape
    q = (rms_norm(cq, g_cq) @ w_uq).reshape(b, l, MLA_HEADS, MLA_QK)
    q = rms_norm(q, g_qh)
    if rope is None:
        return q
    return jnp.concatenate([q[..., :MLA_NOPE], apply_axial_rope(q[..., MLA_NOPE:], *rope)], axis=-1)


def mla_keyval(ckv, krope, g_ckv, w_ukv, g_kh, rope):
    b, l, _ = ckv.shape
    kv = (rms_norm(ckv, g_ckv) @ w_ukv).reshape(b, l, MLA_HEADS, MLA_NOPE + MLA_V)
    k_nope, v = kv[..., :MLA_NOPE], kv[..., MLA_NOPE:]
    k_pe = jnp.broadcast_to(krope[:, :, None, :], (b, l, MLA_HEADS, MLA_ROPE))
    k = rms_norm(jnp.concatenate([k_nope, k_pe], axis=-1), g_kh)
    if rope is not None:
        k = jnp.concatenate([k[..., :MLA_NOPE], apply_axial_rope(k[..., MLA_NOPE:], *rope)], axis=-1)
    return k, v


def block_attention(q, k, v):
    b, lq, h, e = q.shape
    scale = 1.0 / math.sqrt(e)
    qb = q.reshape(b, lq // Q_BLOCK, Q_BLOCK, h, e).transpose(1, 0, 2, 3, 4)

    def one_block(qi):
        s = jnp.einsum('bqhe,bkhe->bhqk', qi, k, preferred_element_type=jnp.float32) * scale
        p = jax.nn.softmax(s, axis=-1).astype(v.dtype)
        return jnp.einsum('bhqk,bkhv->bqhv', p, v)

    out = lax.map(one_block, qb)
    return out.transpose(1, 0, 2, 3, 4).reshape(b, lq, h * v.shape[-1])


def ssd_scan(xh, dt, a, bh, ch, init_state):
    b, l, h, p = xh.shape
    n = bh.shape[-1]
    nc = l // SSM_CHUNK
    xc = xh.astype(jnp.float32).reshape(b, nc, SSM_CHUNK, h, p)
    bc = bh.astype(jnp.float32).reshape(b, nc, SSM_CHUNK, h, n)
    cc = ch.astype(jnp.float32).reshape(b, nc, SSM_CHUNK, h, n)
    dtc = dt.reshape(b, nc, SSM_CHUNK, h)
    cum = jnp.cumsum(dtc * a, axis=2)
    tri = jnp.tril(jnp.ones((SSM_CHUNK, SSM_CHUNK), dtype=bool))[None, None, :, :, None]
    seg = cum[:, :, :, None, :] - cum[:, :, None, :, :]
    decay = jnp.where(tri, jnp.exp(jnp.where(tri, seg, 0.0)), 0.0)
    scores = jnp.einsum('bcihn,bcjhn->bcijh', cc, bc) * decay
    y_diag = jnp.einsum('bcijh,bcjhp->bcihp', scores * dtc[:, :, None, :, :], xc)
    w_end = jnp.exp(cum[:, :, -1:, :] - cum) * dtc
    chunk_states = jnp.einsum('bcjhn,bcjh,bcjhp->bchpn', bc, w_end, xc)
    chunk_decay = jnp.exp(cum[:, :, -1, :])

    def step(s, inp):
        st, dec = inp
        return s * dec[:, :, None, None] + st, s

    final, prev = lax.scan(step, init_state.astype(jnp.float32),
                           (jnp.moveaxis(chunk_states, 1, 0), jnp.moveaxis(chunk_decay, 1, 0)))
    prev = jnp.moveaxis(prev, 0, 1)
    y_off = jnp.einsum('bcihn,bchpn->bcihp', cc, prev) * jnp.exp(cum)[..., None]
    return (y_diag + y_off).reshape(b, l, h, p), final


def ssm_prepare(xbc, dt_raw, w_conv, b_conv, dt_bias):
    b, l, _ = xbc.shape
    u = jax.nn.silu(dw_conv(xbc, w_conv, b_conv))
    rep = SSM_HEADS // SSM_GROUPS
    xs = u[..., :SSM_INNER].reshape(b, l, SSM_HEADS, SSM_HEAD_DIM)
    bs = jnp.repeat(u[..., SSM_INNER:SSM_INNER + SSM_GROUPS * SSM_STATE].reshape(b, l, SSM_GROUPS, SSM_STATE), rep, axis=2)
    cs = jnp.repeat(u[..., SSM_INNER + SSM_GROUPS * SSM_STATE:].reshape(b, l, SSM_GROUPS, SSM_STATE), rep, axis=2)
    dt = jax.nn.softplus(dt_raw.astype(jnp.float32).reshape(b, l, 2, SSM_HEADS) + dt_bias.astype(jnp.float32))
    return xs, bs, cs, dt


def bidir_ssd(xs, bs, cs, dt, a, init_fwd, init_bwd):
    flip = lambda t: jnp.flip(t, axis=1)
    y_f, s_f = ssd_scan(xs, dt[:, :, 0], a[0], bs, cs, init_fwd)
    y_b, s_b = ssd_scan(flip(xs), flip(dt[:, :, 1]), a[1], flip(bs), flip(cs), init_bwd)
    return y_f + flip(y_b), s_f, s_b


def ssm_out(y, xs, z, d_skip, g_norm):
    b, l = y.shape[:2]
    y = y + xs.astype(jnp.float32) * d_skip.astype(jnp.float32)[:, None]
    y = y.reshape(b, l, SSM_INNER) * jax.nn.silu(z.astype(jnp.float32))
    y = rms_norm(y.reshape(b, l, SSM_GROUPS, SSM_INNER // SSM_GROUPS), g_norm.reshape(SSM_GROUPS, -1))
    return y.reshape(b, l, SSM_INNER)


def hyena_filters(length, w_f1, b_f1, freq_f1, w_f2, b_f2, freq_f2, w_f3):
    f32 = jnp.float32
    t01 = jnp.linspace(0.0, 1.0, length, dtype=f32)[:, None]
    w = (2.0 * math.pi / length) * jnp.arange(length, dtype=f32)[:, None]
    bands = jnp.linspace(1e-4, HY_BANDS - 1, HY_BANDS, dtype=f32)[None, :]
    feats = jnp.concatenate([t01, jnp.cos(bands * w), -jnp.sin(bands * w)], axis=-1)
    h = jnp.sin(freq_f1.astype(f32) * (feats @ w_f1.astype(f32) + b_f1.astype(f32)))
    h = jnp.sin(freq_f2.astype(f32) * (h @ w_f2.astype(f32) + b_f2.astype(f32)))
    h = (h @ w_f3.astype(f32)).reshape(length, 2, HY_ORDER, HY_CH)
    deltas = jnp.abs(jnp.linspace(math.log(HY_DECAY_TARGET) / HY_DECAY_PCT_LONG,
                                  math.log(HY_DECAY_TARGET) / HY_DECAY_PCT_SHORT, HY_CH, dtype=f32))
    h = h * jnp.exp(-t01 * deltas)[:, None, None, :]
    filt = jnp.concatenate([h[:, 0], jnp.zeros((1, HY_ORDER, HY_CH), f32), jnp.flip(h[1:, 1], axis=0)], axis=0)
    return filt / (jnp.sum(jnp.abs(filt), axis=0, keepdims=True) + EPS)


def fft_long_conv(u, filt, d):
    l = u.shape[1]
    spec = jnp.fft.rfft(u, n=2 * l, axis=1) * jnp.fft.rfft(filt, axis=0)[None]
    return jnp.fft.irfft(spec, n=2 * l, axis=1)[:, :l] + u * d


def hyena_mixer(proj, w_conv, b_conv, w_f1, b_f1, freq_f1, w_f2, b_f2, freq_f2, w_f3, d_skip):
    l = proj.shape[1]
    parts = jnp.split(dw_conv(proj, w_conv, b_conv).astype(jnp.float32), HY_ORDER + 1, axis=-1)
    filt = hyena_filters(l, w_f1, b_f1, freq_f1, w_f2, b_f2, freq_f2, w_f3)
    z = parts[0]
    for o in range(HY_ORDER):
        z = parts[o + 1] * fft_long_conv(z, filt[:, o], d_skip[o].astype(jnp.float32))
    return z


def sq_relu_mlp(h, w1, w2):
    return jnp.square(jax.nn.relu(h @ w1)) @ w2


def setup_inputs(seed: int = 0) -> dict:
    f32 = jnp.float32
    keys = iter(jax.random.split(jax.random.key(seed), 40))

    def normal(shape, scale):
        return jax.random.normal(next(keys), shape, f32) * scale

    def gain(shape):
        return 1.0 + 0.05 * jax.random.normal(next(keys), shape, f32)

    dt0 = jnp.exp(jax.random.uniform(next(keys), (DEPTH, 2, SSM_HEADS), f32, math.log(1e-3), math.log(1e-1)))
    a0 = jax.random.uniform(next(keys), (DEPTH, 2, SSM_HEADS), f32, 1.0, 16.0)
    return {
        'x': normal((BATCH, SEQ, D_MODEL), 1.0),
        'c': normal((BATCH, D_MODEL), 1.0),
        'ctx': normal((BATCH, CTX_LEN, D_MODEL), 1.0),
        'c_ctx': normal((D_MODEL,), 1.0),
        'w_mod': normal((DEPTH, D_MODEL, 6 * D_MODEL), D_MODEL ** -0.5),
        'b_mod': normal((DEPTH, 6 * D_MODEL), 0.02),
        'g_norm_mix': gain((DEPTH, D_MODEL)),
        'g_norm_mlp': gain((DEPTH, D_MODEL)),
        'w_in': normal((DEPTH, D_MODEL, IN_COLS), D_MODEL ** -0.5),
        'w_out': normal((DEPTH, D_MIX, D_MODEL), D_MIX ** -0.5),
        'g_cq': gain((DEPTH, MLA_Q_LORA)),
        'g_ckv': gain((DEPTH, MLA_KV_LORA)),
        'w_uq': normal((DEPTH, MLA_Q_LORA, MLA_HEADS * MLA_QK), MLA_Q_LORA ** -0.5),
        'w_ukv': normal((DEPTH, MLA_KV_LORA, MLA_HEADS * (MLA_NOPE + MLA_V)), MLA_KV_LORA ** -0.5),
        'g_qhead': gain((DEPTH, MLA_QK)),
        'g_khead': gain((DEPTH, MLA_QK)),
        'w_conv_ssm': normal((DEPTH, SSM_CONV, SSM_XBC), SSM_CONV ** -0.5),
        'b_conv_ssm': normal((DEPTH, SSM_XBC), 0.02),
        'a_log': jnp.log(a0),
        'dt_bias': dt0 + jnp.log(-jnp.expm1(-dt0)),
        'd_skip_ssm': gain((DEPTH, SSM_HEADS)),
        'g_ssm_out': gain((DEPTH, SSM_INNER)),
        'w_conv_hy': normal((DEPTH, HY_CONV, (HY_ORDER + 1) * HY_CH), HY_CONV ** -0.5),
        'b_conv_hy': normal((DEPTH, (HY_ORDER + 1) * HY_CH), 0.02),
        'w_f1': normal((DEPTH, HY_EMB, HY_HIDDEN), HY_EMB ** -0.5),
        'b_f1': normal((DEPTH, HY_HIDDEN), 0.02),
        'freq_f1': gain((DEPTH, HY_HIDDEN)),
        'w_f2': normal((DEPTH, HY_HIDDEN, HY_HIDDEN), HY_HIDDEN ** -0.5),
        'b_f2': normal((DEPTH, HY_HIDDEN), 0.02),
        'freq_f2': gain((DEPTH, HY_HIDDEN)),
        'w_f3': normal((DEPTH, HY_HIDDEN, 2 * HY_ORDER * HY_CH), HY_HIDDEN ** -0.5),
        'd_skip_hy': normal((DEPTH, HY_ORDER, HY_CH), 1.0),
        'w_ff1': normal((DEPTH, D_MODEL, D_FF), D_MODEL ** -0.5),
        'w_ff2': normal((DEPTH, D_FF, D_MODEL), D_FF ** -0.5),
    }


def reference(x, c, ctx, c_ctx, w_mod, b_mod, g_norm_mix, g_norm_mlp, w_in, w_out,
              g_cq, g_ckv, w_uq, w_ukv, g_qhead, g_khead,
              w_conv_ssm, b_conv_ssm, a_log, dt_bias, d_skip_ssm, g_ssm_out,
              w_conv_hy, b_conv_hy, w_f1, b_f1, freq_f1, w_f2, b_f2, freq_f2, w_f3, d_skip_hy,
              w_ff1, w_ff2):
    bsz, seq, _ = x.shape
    rope_l = axial_rope_tables(seq)
    xl, xc = x, ctx
    for i in range(DEPTH):
        last = i == DEPTH - 1
        mod_l = jax.nn.silu(c) @ w_mod[i] + b_mod[i]
        mod_c = jax.nn.silu(c_ctx) @ w_mod[i] + b_mod[i]
        sh1_l, sc1_l, ga1_l, sh2_l, sc2_l, ga2_l = jnp.split(mod_l[:, None, :], 6, axis=-1)
        sh1_c, sc1_c, ga1_c, sh2_c, sc2_c, ga2_c = jnp.split(mod_c, 6, axis=-1)

        hl = modulate(rms_norm(xl, g_norm_mix[i]), sh1_l, sc1_l)
        hc = modulate(rms_norm(xc, g_norm_mix[i]), sh1_c, sc1_c)
        cq_l, ckv_l, kr_l, z_l, xbc_l, dt_l, hy_l = jnp.split(hl @ w_in[i], IN_OFFSETS, axis=-1)
        cq_c, ckv_c, kr_c, z_c, xbc_c, dt_c, hy_c = jnp.split(hc @ w_in[i], IN_OFFSETS, axis=-1)

        k_c, v_c = mla_keyval(ckv_c, kr_c, g_ckv[i], w_ukv[i], g_khead[i], None)
        k_l, v_l = mla_keyval(ckv_l, kr_l, g_ckv[i], w_ukv[i], g_khead[i], rope_l)
        q_l = mla_query(cq_l, g_cq[i], w_uq[i], g_qhead[i], rope_l)
        att_l = block_attention(q_l, jnp.concatenate([k_c, k_l], axis=1), jnp.concatenate([v_c, v_l], axis=1))

        a = -jnp.exp(a_log[i].astype(jnp.float32))
        xs_c, bs_c, cs_c, dts_c = ssm_prepare(xbc_c, dt_c, w_conv_ssm[i], b_conv_ssm[i], dt_bias[i])
        xs_l, bs_l, cs_l, dts_l = ssm_prepare(xbc_l, dt_l, w_conv_ssm[i], b_conv_ssm[i], dt_bias[i])
        zero = jnp.zeros((bsz, SSM_HEADS, SSM_HEAD_DIM, SSM_STATE), jnp.float32)
        y_c, s_fwd, s_bwd = bidir_ssd(xs_c, bs_c, cs_c, dts_c, a, zero, zero)
        y_l, _, _ = bidir_ssd(xs_l, bs_l, cs_l, dts_l, a, s_fwd, s_bwd)
        ssm_l = ssm_out(y_l, xs_l, z_l, d_skip_ssm[i], g_ssm_out[i])

        hyp = (w_conv_hy[i], b_conv_hy[i], w_f1[i], b_f1[i], freq_f1[i], w_f2[i], b_f2[i], freq_f2[i], w_f3[i], d_skip_hy[i])
        hyo_l = hyena_mixer(hy_l, *hyp)

        mix_l = jnp.concatenate([att_l, ssm_l.astype(xl.dtype), hyo_l.astype(xl.dtype)], axis=-1) @ w_out[i]
        xl = xl + ga1_l * mix_l
        xl = xl + ga2_l * sq_relu_mlp(modulate(rms_norm(xl, g_norm_mlp[i]), sh2_l, sc2_l), w_ff1[i], w_ff2[i])

        if not last:
            q_c = mla_query(cq_c, g_cq[i], w_uq[i], g_qhead[i], None)
            att_c = block_attention(q_c, k_c, v_c)
            ssm_c = ssm_out(y_c, xs_c, z_c, d_skip_ssm[i], g_ssm_out[i])
            hyo_c = hyena_mixer(hy_c, *hyp)
            mix_c = jnp.concatenate([att_c, ssm_c.astype(xc.dtype), hyo_c.astype(xc.dtype)], axis=-1) @ w_out[i]
            xc = xc + ga1_c * mix_c
            xc = xc + ga2_c * sq_relu_mlp(modulate(rms_norm(xc, g_norm_mlp[i]), sh2_c, sc2_c), w_ff1[i], w_ff2[i])
    return xl
```

```python
import functools
import math

import jax
import jax.numpy as jnp
import numpy as np
from jax import lax
from jax.experimental import pallas as pl
from jax.experimental.pallas import tpu as pltpu

F32 = jnp.float32
BF16 = jnp.bfloat16
HIGHEST = lax.Precision.HIGHEST

D_MODEL = 1024
DEPTH = 2
GRID_W = 64
EPS = 1e-6
N_HEADS = 6
D_NOPE = 64
D_ROPE = 32
D_QK = D_NOPE + D_ROPE
D_V = 64
Q_LORA = 256
KV_LORA = 128
ROPE_THETA = 10000.0
ROPE_FREQS = D_ROPE // 4
S_HEADS = 6
S_HDIM = 64
S_INNER = S_HEADS * S_HDIM
S_GROUPS = 2
S_STATE = 64
S_XBC = S_INNER + 2 * S_GROUPS * S_STATE
HY_CH = D_MODEL - N_HEADS * D_V - S_INNER
HY_BANDS = 16
HY_EMB = 1 + 2 * HY_BANDS
HY_HIDDEN = 64
HY_DECAY_PCT_SHORT = 0.3
HY_DECAY_PCT_LONG = 1.5
HY_DECAY_TARGET = 1e-2
D_FF = 4 * D_MODEL

LANES = 128
SUBLANES = 8
VMEM_LIMIT = 56 * 1024 * 1024

TM = 256
CHUNK = 128
HEAD_PAD = LANES
HY_BLOCK = 1024
HY_FT = 256


def _cparams(*sem):
    return pltpu.CompilerParams(dimension_semantics=sem, vmem_limit_bytes=VMEM_LIMIT)


def _dot(a, b, precision=None):
    return jnp.dot(a, b, preferred_element_type=F32, precision=precision)


def _dot_nt(a, b):
    return lax.dot_general(a, b, (((1,), (1,)), ((), ())), preferred_element_type=F32)


def _dot_tn(a, b):
    return lax.dot_general(a, b, (((0,), (0,)), ((), ())), preferred_element_type=F32)


def _rms(x):
    return x * lax.rsqrt(jnp.mean(x * x, axis=-1, keepdims=True) + EPS)


def _sigmoid(x):
    return 1.0 / (1.0 + jnp.exp(-x))


def _mod_kernel(c_ref, w_ref, b_ref, o_ref):
    cv = c_ref[...]
    s = (cv * _sigmoid(cv)).astype(BF16)
    o_ref[0] = _dot(s, w_ref[0].astype(BF16)) + b_ref[0]


def _modulation(cvec, w_mod, b_mod):
    tn = 1024
    ncol = w_mod.shape[-1]
    return pl.pallas_call(
        _mod_kernel,
        out_shape=jax.ShapeDtypeStruct((DEPTH, SUBLANES, ncol), F32),
        grid=(DEPTH, ncol // tn),
        in_specs=[pl.BlockSpec((SUBLANES, D_MODEL), lambda l, j: (0, 0)),
                  pl.BlockSpec((1, D_MODEL, tn), lambda l, j: (l, 0, j)),
                  pl.BlockSpec((1, 1, tn), lambda l, j: (l, 0, j))],
        out_specs=pl.BlockSpec((1, SUBLANES, tn), lambda l, j: (l, 0, j)),
        compiler_params=_cparams("parallel", "parallel"),
        name="modulation",
    )(cvec, w_mod, b_mod.reshape(DEPTH, 1, ncol))


def _inproj_kernel(x_ref, sh_ref, sc_ref, g_ref, wa_ref, wz_ref, wx_ref, wdt_ref, whyt_ref,
                   gcq_ref, gckv_ref, wuq_ref, wk_ref, wv_ref, gq_ref, gk_ref, ct_ref, sa_ref, sb_ref,
                   q_ref, k_ref, v_ref, z_ref, xbc_ref, dt_ref, hyt_ref):
    x = x_ref[0]
    h = _rms(x) * g_ref[...]
    h = h * (1.0 + sc_ref[0]) + sh_ref[0]
    hb = h.astype(BF16)
    z_ref[0] = _dot(hb, wz_ref[...])
    xbc_ref[0] = _dot(hb, wx_ref[...])
    dt_ref[0] = _dot(hb, wdt_ref[...])
    hyt_ref[0] = _dot_nt(whyt_ref[...], hb)
    a = _dot(hb, wa_ref[...])
    cq = a[:, :Q_LORA]
    ckv = a[:, Q_LORA:Q_LORA + KV_LORA]
    krb = a[:, Q_LORA + KV_LORA:]
    cqn = (_rms(cq) * gcq_ref[...]).astype(BF16)
    ckvn = (_rms(ckv) * gckv_ref[...]).astype(BF16)
    qr = _dot(cqn, wuq_ref[...])
    kn = _dot(ckvn, wk_ref[...])
    v_ref[0] = _dot(ckvn, wv_ref[...]).astype(BF16)
    ct = ct_ref[...]
    sa = sa_ref[...]
    sb = sb_ref[...]
    gq = gq_ref[...]
    gk = gk_ref[...]

    def head_norm_rope(t, g):
        ss = jnp.sum(t * t, axis=-1, keepdims=True) * (1.0 / D_QK)
        t = t * lax.rsqrt(ss + EPS) * g
        return t * ct + pltpu.roll(t, HEAD_PAD - ROPE_FREQS, 1) * sa + pltpu.roll(t, ROPE_FREQS, 1) * sb

    qscale = 1.0 / math.sqrt(D_QK)
    for hh in range(N_HEADS):
        sl = slice(HEAD_PAD * hh, HEAD_PAD * (hh + 1))
        q_ref[0, :, sl] = (head_norm_rope(qr[:, sl], gq) * qscale).astype(BF16)
        k_ref[0, :, sl] = head_norm_rope(kn[:, sl] + krb, gk).astype(BF16)


def _inproj(xall, modtok, layer, lw, rope_tabs):
    bsz, ntok, _ = xall.shape
    nt = ntok // TM
    hw = N_HEADS * HEAD_PAD

    def tok(width, dtype):
        return jax.ShapeDtypeStruct((bsz, ntok, width), dtype)

    def modspec(chunk):
        return pl.BlockSpec(
            (1, 1, D_MODEL),
            lambda b, i: (layer * 2 * bsz + 2 * b + jnp.where(i == nt - 1, 1, 0), 0, chunk))

    def full(arr):
        return pl.BlockSpec(arr.shape, lambda b, i: (0,) * arr.ndim)

    def tokspec(width):
        return pl.BlockSpec((1, TM, width), lambda b, i: (b, i, 0))

    weights = [lw["g_mix"], lw["wa"], lw["wz"], lw["wx"], lw["wdt"], lw["whyt"], lw["g_cq"], lw["g_ckv"],
               lw["wuq"], lw["wk"], lw["wv"], lw["g_q"], lw["g_k"]]
    tabspec = pl.BlockSpec((TM, HEAD_PAD), lambda b, i: (i, 0))
    return pl.pallas_call(
        _inproj_kernel,
        out_shape=[tok(hw, BF16), tok(hw, BF16), tok(hw, BF16), tok(S_INNER, F32), tok(S_XBC, F32),
                   tok(LANES, F32), jax.ShapeDtypeStruct((bsz, 3 * HY_CH, ntok), F32)],
        grid=(bsz, nt),
        in_specs=[tokspec(D_MODEL), modspec(0), modspec(1)] + [full(w) for w in weights] + [tabspec] * 3,
        out_specs=[tokspec(hw), tokspec(hw), tokspec(hw), tokspec(S_INNER), tokspec(S_XBC), tokspec(LANES),
                   pl.BlockSpec((1, 3 * HY_CH, TM), lambda b, i: (b, 0, i))],
        compiler_params=_cparams("parallel", "parallel"),
        name="inproj",
    )(xall, modtok, modtok, *weights, *rope_tabs)


def _attn_kernel(q_ref, k_ref, v_ref, o_ref, *, k_tiles):
    tq = q_ref.shape[1]
    outs = []
    for hh in range(2):
        sl = slice(HEAD_PAD * hh, HEAD_PAD * (hh + 1))
        q = q_ref[0, :, sl]
        m = jnp.full((tq, 1), -jnp.inf, F32)
        l = jnp.zeros((tq, 1), F32)
        acc = jnp.zeros((tq, HEAD_PAD), F32)
        for start, size in k_tiles:
            s = _dot_nt(q, k_ref[0, start:start + size, sl])
            m_new = jnp.maximum(m, jnp.max(s, axis=-1, keepdims=True))
            p = jnp.exp(s - m_new)
            alpha = jnp.exp(m - m_new)
            l = alpha * l + jnp.sum(p, axis=-1, keepdims=True)
            acc = alpha * acc + _dot(p.astype(BF16), v_ref[0, start:start + size, sl])
            m = m_new
        outs.append(acc / l)
    o_ref[0] = (outs[0] + outs[1]).astype(BF16)


def _attention(q, k, v, q_tile0, n_qtiles, k_row0, k_rows, tq, k_tile):
    bsz = q.shape[0]
    assert k_row0 % k_rows == 0
    kb = k_row0 // k_rows
    k_tiles = []
    pos = 0
    while pos < k_rows:
        size = min(k_tile, k_rows - pos)
        k_tiles.append((pos, size))
        pos += size
    pw = 2 * HEAD_PAD
    return pl.pallas_call(
        functools.partial(_attn_kernel, k_tiles=tuple(k_tiles)),
        out_shape=jax.ShapeDtypeStruct((bsz, n_qtiles * tq, N_HEADS * D_V), BF16),
        grid=(bsz, N_HEADS // 2, n_qtiles),
        in_specs=[pl.BlockSpec((1, tq, pw), lambda b, p, i: (b, q_tile0 + i, p)),
                  pl.BlockSpec((1, k_rows, pw), lambda b, p, i: (b, kb, p)),
                  pl.BlockSpec((1, k_rows, pw), lambda b, p, i: (b, kb, p))],
        out_specs=pl.BlockSpec((1, tq, 2 * D_V), lambda b, p, i: (b, i, p)),
        compiler_params=_cparams("parallel", "parallel", "parallel"),
        name="attention",
    )(q, k, v)


def _ssmprep_kernel(x_ref, prev_ref, next_ref, dtr_ref, w_ref, b_ref, dtb_ref, u_ref, dt_ref, *, nt):
    i = pl.program_id(1)
    x = x_ref[0]
    first = jnp.logical_or(i == 0, i == nt - 1)
    last = i >= nt - 2
    pv = prev_ref[0][SUBLANES - 1:SUBLANES, :] * jnp.where(first, 0.0, 1.0)
    nx = next_ref[0][0:1, :] * jnp.where(last, 0.0, 1.0)
    rows = lax.broadcasted_iota(jnp.int32, x.shape, 0)
    up = jnp.where(rows == 0, pv, pltpu.roll(x, 1, 0))
    dn = jnp.where(rows == TM - 1, nx, pltpu.roll(x, TM - 1, 0))
    w = w_ref[...]
    y = up * w[0:1] + x * w[1:2] + dn * w[2:3] + b_ref[...]
    u_ref[0] = y * _sigmoid(y)
    t = dtr_ref[0] + dtb_ref[...]
    dt_ref[0] = jnp.maximum(t, 0.0) + jnp.log(1.0 + jnp.exp(-jnp.abs(t)))


def _ssmprep(xbc, dtraw, w_conv, b_conv, dt_bias):
    bsz, ntok, width = xbc.shape
    nt = ntok // TM
    r = TM // SUBLANES
    nblk8 = ntok // SUBLANES
    wpad = jnp.zeros((SUBLANES, width), F32).at[:3].set(w_conv)
    dtb = jnp.zeros((1, LANES), F32).at[0, :2 * S_HEADS].set(dt_bias.reshape(-1))
    return pl.pallas_call(
        functools.partial(_ssmprep_kernel, nt=nt),
        out_shape=[jax.ShapeDtypeStruct(xbc.shape, F32), jax.ShapeDtypeStruct(dtraw.shape, F32)],
        grid=(bsz, nt),
        in_specs=[pl.BlockSpec((1, TM, width), lambda b, i: (b, i, 0)),
                  pl.BlockSpec((1, SUBLANES, width), lambda b, i: (b, jnp.maximum(i * r - 1, 0), 0)),
                  pl.BlockSpec((1, SUBLANES, width), lambda b, i: (b, jnp.minimum((i + 1) * r, nblk8 - 1), 0)),
                  pl.BlockSpec((1, TM, LANES), lambda b, i: (b, i, 0)),
                  pl.BlockSpec((SUBLANES, width), lambda b, i: (0, 0)),
                  pl.BlockSpec((1, width), lambda b, i: (0, 0)),
                  pl.BlockSpec((1, LANES), lambda b, i: (0, 0))],
        out_specs=[pl.BlockSpec((1, TM, width), lambda b, i: (b, i, 0)),
                   pl.BlockSpec((1, TM, LANES), lambda b, i: (b, i, 0))],
        compiler_params=_cparams("parallel", "parallel"),
        name="ssm_prep",
    )(xbc, xbc, xbc, dtraw, wpad, b_conv.reshape(1, width), dtb)


def _ssd_kernel(u_ref, dt_ref, a_ref, e_ref, bm_ref, y_ref, st_ref, *, reverse):
    @pl.when(pl.program_id(1) == 0)
    def _():
        st_ref[...] = jnp.zeros_like(st_ref)

    u = u_ref[0]
    dt = dt_ref[0]
    xs = u[:, :S_INNER]
    bmat = u[:, S_INNER:S_INNER + LANES]
    cmat = u[:, S_INNER + LANES:]
    off = S_HEADS if reverse else 0
    dta = dt * a_ref[...]
    ii = lax.broadcasted_iota(jnp.int32, (CHUNK, CHUNK), 0)
    jj = lax.broadcasted_iota(jnp.int32, (CHUNK, CHUNK), 1)
    tri = (jj >= ii) if reverse else (jj <= ii)
    cum = _dot(tri.astype(F32), dta, HIGHEST)
    cum_t = cum.T
    expand = e_ref[...]
    cum_e = _dot(cum, expand, HIGHEST)
    dt_e = _dot(dt, expand, HIGHEST)
    tot_e = cum_e[0:1] if reverse else cum_e[CHUNK - 1:CHUNK]
    xdt = xs * dt_e
    st = st_ref[...]
    lane = lax.broadcasted_iota(jnp.int32, (CHUNK, LANES), 1)
    cb16 = cmat.astype(BF16)
    y_off = _dot(cb16, st.astype(BF16)) * jnp.exp(cum_e)
    new = _dot(bmat.T.astype(BF16), (xdt * jnp.exp(tot_e - cum_e)).astype(BF16))
    st_ref[...] = (st * jnp.exp(tot_e) + new) * bm_ref[...]
    b16 = bmat.astype(BF16)
    cbs = [_dot_nt(jnp.where((lane // S_STATE) == g, cmat, 0.0).astype(BF16), b16) for g in range(S_GROUPS)]
    for pair in range(S_HEADS // 2):
        xp = xdt[:, LANES * pair:LANES * (pair + 1)]
        acc = None
        for half in range(2):
            hh = 2 * pair + half
            seg = cum[:, off + hh:off + hh + 1] - cum_t[off + hh:off + hh + 1, :]
            dec = jnp.where(tri, jnp.exp(jnp.where(tri, seg, 0.0)), 0.0)
            sc = (cbs[hh // (S_HEADS // S_GROUPS)] * dec).astype(BF16)
            xh = jnp.where((lane // S_HDIM) == half, xp, 0.0).astype(BF16)
            part = _dot(sc, xh)
            acc = part if acc is None else acc + part
        sl = slice(LANES * pair, LANES * (pair + 1))
        y_ref[0, :, sl] = acc + y_off[:, sl]


def _ssd(u, dt, a_row, expand, blockmask, reverse):
    bsz, ntok, width = u.shape
    nchunk = ntok // CHUNK
    nctx = TM // CHUNK
    nlat = nchunk - nctx

    if reverse:
        def cidx(s):
            return jnp.where(s < nctx, nchunk - 1 - s, nlat - 1 - (s - nctx))
    else:
        def cidx(s):
            return jnp.where(s < nctx, nlat + s, s - nctx)

    return pl.pallas_call(
        functools.partial(_ssd_kernel, reverse=reverse),
        out_shape=jax.ShapeDtypeStruct((bsz, ntok, S_INNER), F32),
        grid=(bsz, nchunk),
        in_specs=[pl.BlockSpec((1, CHUNK, width), lambda b, s: (b, cidx(s), 0)),
                  pl.BlockSpec((1, CHUNK, LANES), lambda b, s: (b, cidx(s), 0)),
                  pl.BlockSpec((1, LANES), lambda b, s: (0, 0)),
                  pl.BlockSpec((LANES, S_INNER), lambda b, s: (0, 0)),
                  pl.BlockSpec((S_GROUPS * S_STATE, S_INNER), lambda b, s: (0, 0))],
        out_specs=pl.BlockSpec((1, CHUNK, S_INNER), lambda b, s: (b, cidx(s), 0)),
        scratch_shapes=[pltpu.VMEM((S_GROUPS * S_STATE, S_INNER), F32)],
        compiler_params=_cparams("parallel", "arbitrary"),
        name="ssd_bwd" if reverse else "ssd_fwd",
    )(u, dt, a_row, expand, blockmask)


def _hyprep_kernel(x_ref, w_ref, o_ref):
    x = x_ref[0]
    n = x.shape[1]
    t = lax.broadcasted_iota(jnp.int32, x.shape, 1)
    up = jnp.where(t == 0, 0.0, pltpu.roll(x, 1, 1))
    dn = jnp.where(t == n - 1, 0.0, pltpu.roll(x, n - 1, 1))
    w = w_ref[...]
    o_ref[0] = up * w[:, 0:1] + x * w[:, 1:2] + dn * w[:, 2:3] + w[:, 3:4]


def _hyprep(hyt, wcols, lane_blk, lane_idx):
    bsz, rows, _ = hyt.shape
    rt = 256
    return pl.pallas_call(
        _hyprep_kernel,
        out_shape=jax.ShapeDtypeStruct((bsz, rows, lane_blk), F32),
        grid=(bsz, rows // rt),
        in_specs=[pl.BlockSpec((1, rt, lane_blk), lambda b, r: (b, r, lane_idx)),
                  pl.BlockSpec((rt, LANES), lambda b, r: (r, 0))],
        out_specs=pl.BlockSpec((1, rt, lane_blk), lambda b, r: (b, r, 0)),
        compiler_params=_cparams("parallel", "parallel"),
        name="hyena_prep",
    )(hyt, wcols)


def _hyfilt_kernel(ft_ref, w1_ref, b1_ref, q1_ref, w2_ref, b2_ref, q2_ref, w3b_ref, w3f_ref, dl_ref, o_ref):
    ft = ft_ref[...]
    n = ft.shape[1]
    half = n // 2
    h = jnp.sin(q1_ref[...] * (_dot(w1_ref[...], ft, HIGHEST) + b1_ref[...]))
    h = jnp.sin(q2_ref[...] * (_dot(w2_ref[...], h, HIGHEST) + b2_ref[...]))
    fb = _dot(w3b_ref[...], h[:, :half], HIGHEST)
    ff = _dot(w3f_ref[...], h[:, half:], HIGHEST)
    f = jnp.concatenate([fb, ff], axis=1)
    f = f * jnp.exp(-ft[0:1, :] * dl_ref[...])
    pos = lax.broadcasted_iota(jnp.int32, f.shape, 1)
    f = jnp.where(pos == 0, 0.0, f)
    o_ref[...] = f / (jnp.sum(jnp.abs(f), axis=1, keepdims=True) + EPS)


def _hyfilt(feats_t, hw):
    n = feats_t.shape[1]
    rows = 2 * HY_CH
    rt = 128
    args = [feats_t, hw["w1t"], hw["b1"], hw["q1"], hw["w2t"], hw["b2"], hw["q2"]]
    return pl.pallas_call(
        _hyfilt_kernel,
        out_shape=jax.ShapeDtypeStruct((rows, n), F32),
        grid=(rows // rt,),
        in_specs=[pl.BlockSpec(a.shape, lambda r: (0, 0)) for a in args]
        + [pl.BlockSpec((rt, HY_HIDDEN), lambda r: (r, 0)), pl.BlockSpec((rt, HY_HIDDEN), lambda r: (r, 0)),
           pl.BlockSpec((rt, 1), lambda r: (r, 0))],
        out_specs=pl.BlockSpec((rt, n), lambda r: (r, 0)),
        compiler_params=_cparams("parallel"),
        name="hyena_filter",
    )(*args, hw["w3b"], hw["w3f"], hw["delta"])


def _hyspec_kernel(f_ref, c_ref, s_ref, g_ref, *, blk, nb):
    kt = pl.program_id(0)
    ftile = c_ref.shape[1]
    ctab = c_ref[...]
    stab = s_ref[...]
    freq = lax.broadcasted_iota(jnp.int32, (2 * HY_CH, ftile), 1) + kt * ftile
    sigma = jnp.where((freq & 1) == 0, 1.0, -1.0)
    scale = 2.0 / (2 * blk)
    prev = None
    for e in range(2 * nb):
        phi = f_ref[:, e * blk:(e + 1) * blk]
        p16 = phi.astype(BF16)
        a = _dot(p16, ctab)
        bs = _dot(p16, stab)
        cur = (a, bs, phi[:, 0:1])
        if prev is not None:
            gr = (a + sigma * prev[1]) * scale
            gi = (sigma * (prev[0] - prev[2]) - bs) * scale
            for o in range(2):
                g_ref[o, e - 1, 0] = gr[o * HY_CH:(o + 1) * HY_CH]
                g_ref[o, e - 1, 1] = gi[o * HY_CH:(o + 1) * HY_CH]
        prev = cur


def _hyspec(filt, ctab, stab, blk):
    n = filt.shape[1]
    nb = n // (2 * blk)
    ft = min(HY_FT, blk)
    return pl.pallas_call(
        functools.partial(_hyspec_kernel, blk=blk, nb=nb),
        out_shape=jax.ShapeDtypeStruct((2, 2 * nb - 1, 2, HY_CH, blk), F32),
        grid=(blk // ft,),
        in_specs=[pl.BlockSpec(filt.shape, lambda k: (0, 0)),
                  pl.BlockSpec((blk, ft), lambda k: (0, k)),
                  pl.BlockSpec((blk, ft), lambda k: (0, k))],
        out_specs=pl.BlockSpec((2, 2 * nb - 1, 2, HY_CH, ft), lambda k: (0, 0, 0, 0, k)),
        compiler_params=_cparams("parallel"),
        name="hyena_spectra",
    )(filt, ctab, stab)


def _hyconv_kernel(u_ref, m_ref, d_ref, g_ref, c_ref, s_ref, ci_ref, si_ref, o_ref, ub_ref, acc_ref, *, blk, nb):
    kt = pl.program_id(1)

    @pl.when(kt == 0)
    def _():
        ub_ref[...] = u_ref[0, 0].astype(BF16)
        acc_ref[...] = jnp.zeros_like(acc_ref)

    ctab = c_ref[...]
    stab = s_ref[...]
    xr = []
    xs = []
    for j in range(nb):
        uj = ub_ref[:, j * blk:(j + 1) * blk]
        xr.append(_dot(uj, ctab))
        xs.append(_dot(uj, stab))
    cinv = ci_ref[...]
    sinv = si_ref[...]
    for i in range(nb):
        yr = None
        ys = None
        for j in range(nb):
            d = i - j + nb - 1
            gr = g_ref[0, d, 0]
            gi = g_ref[0, d, 1]
            tr = gr * xr[j] + gi * xs[j]
            ts = gr * xs[j] - gi * xr[j]
            yr = tr if yr is None else yr + tr
            ys = ts if ys is None else ys + ts
        acc_ref[:, i * blk:(i + 1) * blk] += _dot(yr.astype(BF16), cinv) + _dot(ys.astype(BF16), sinv)

    @pl.when(kt == pl.num_programs(1) - 1)
    def _():
        o_ref[0] = (m_ref[0, 0] * (acc_ref[...] + u_ref[0, 0] * d_ref[...])).astype(o_ref.dtype)


def _hyconv(u4, usel, m4, msel, dcol, gspec, order, tabs, blk, out_dtype):
    bsz, _, _, n = u4.shape
    nb = n // blk
    ft = min(HY_FT, blk)
    ctab, stab, cinv, sinv = tabs
    return pl.pallas_call(
        functools.partial(_hyconv_kernel, blk=blk, nb=nb),
        out_shape=jax.ShapeDtypeStruct((bsz, HY_CH, n), out_dtype),
        grid=(bsz, blk // ft),
        in_specs=[pl.BlockSpec((1, 1, HY_CH, n), lambda b, k: (b, usel, 0, 0)),
                  pl.BlockSpec((1, 1, HY_CH, n), lambda b, k: (b, msel, 0, 0)),
                  pl.BlockSpec((HY_CH, 1), lambda b, k: (0, 0)),
                  pl.BlockSpec((1, 2 * nb - 1, 2, HY_CH, ft), lambda b, k: (order, 0, 0, 0, k)),
                  pl.BlockSpec((blk, ft), lambda b, k: (0, k)),
                  pl.BlockSpec((blk, ft), lambda b, k: (0, k)),
                  pl.BlockSpec((ft, blk), lambda b, k: (k, 0)),
                  pl.BlockSpec((ft, blk), lambda b, k: (k, 0))],
        out_specs=pl.BlockSpec((1, HY_CH, n), lambda b, k: (b, 0, 0)),
        scratch_shapes=[pltpu.VMEM((HY_CH, n), BF16), pltpu.VMEM((HY_CH, n), F32)],
        compiler_params=_cparams("parallel", "arbitrary"),
        name="hyena_conv",
    )(u4, m4, dcol, gspec, ctab, stab, cinv, sinv)


def _dft_tables(blk):
    s = jnp.arange(blk, dtype=jnp.int32)[:, None]
    k = jnp.arange(blk, dtype=jnp.int32)[None, :]
    phase = (s * (2 * k + 1)) % (4 * blk)
    ang = phase.astype(F32) * (2.0 * math.pi / (4 * blk))
    c = jnp.cos(ang)
    sn = jnp.sin(ang)
    return c.astype(BF16), sn.astype(BF16), c.T.astype(BF16), sn.T.astype(BF16)


def _hyena_features(length):
    step = 1.0 / (length - 1)
    p = jnp.arange(length, dtype=jnp.int32)
    tb = jnp.where(p == 0, 0, length - p)
    tf = p
    t = jnp.concatenate([tb, tf]).astype(F32)
    t01 = t * step
    w = (2.0 * math.pi / length) * t
    bands = jnp.linspace(1e-4, HY_BANDS - 1, HY_BANDS, dtype=F32)[:, None]
    feats = jnp.concatenate([t01[None, :], jnp.cos(bands * w[None, :]), -jnp.sin(bands * w[None, :])], axis=0)
    pad = (-feats.shape[0]) % SUBLANES
    return jnp.pad(feats, ((0, pad), (0, 0)))


def _hyena(hyt, lane_blk, lane_idx, hw, tabs, blk, out_dtype=BF16):
    p = _hyprep(hyt, hw["wconv"], lane_blk, lane_idx)
    bsz = p.shape[0]
    filt = _hyfilt(_hyena_features(lane_blk), hw)
    gspec = _hyspec(filt, tabs[0], tabs[1], blk)
    p4 = p.reshape(bsz, 3, HY_CH, lane_blk)
    z1 = _hyconv(p4, 0, p4, 1, hw["d0"], gspec, 0, tabs, blk, F32)
    return _hyconv(z1.reshape(bsz, 1, HY_CH, lane_blk), 0, p4, 2, hw["d1"], gspec, 1, tabs, blk, out_dtype)


def _post_kernel(x_ref, att_ref, yf_ref, yb_ref, xs_ref, z_ref, hyo_ref, ga1_ref, sh2_ref, sc2_ref, ga2_ref,
                 dsk_ref, gss_ref, wo_ref, g2_ref, w1_ref, w2_ref, o_ref):
    x = x_ref[0]
    y = yf_ref[0] + yb_ref[0] + xs_ref[0] * dsk_ref[...]
    zz = z_ref[0]
    y = y * (zz * _sigmoid(zz))
    gw = S_INNER // S_GROUPS
    lane = lax.broadcasted_iota(jnp.int32, y.shape, 1)
    first = lane < gw
    y2 = y * y
    s0 = jnp.sum(jnp.where(first, y2, 0.0), axis=-1, keepdims=True)
    s1 = jnp.sum(y2, axis=-1, keepdims=True) - s0
    inv = jnp.where(first, lax.rsqrt(s0 * (1.0 / gw) + EPS), lax.rsqrt(s1 * (1.0 / gw) + EPS))
    ssm = (y * inv * gss_ref[...]).astype(BF16)
    na = N_HEADS * D_V
    mix = _dot(att_ref[0], wo_ref[0:na]) + _dot(ssm, wo_ref[na:na + S_INNER])
    mix = mix + _dot_tn(hyo_ref[0], wo_ref[na + S_INNER:])
    x1 = x + ga1_ref[0] * mix
    h2 = (_rms(x1) * g2_ref[...]) * (1.0 + sc2_ref[0]) + sh2_ref[0]
    hb = h2.astype(BF16)
    acc = jnp.zeros_like(x1)
    fc = 1024
    for c in range(D_FF // fc):
        t = jnp.maximum(_dot(hb, w1_ref[:, c * fc:(c + 1) * fc]), 0.0)
        acc = acc + _dot((t * t).astype(BF16), w2_ref[c * fc:(c + 1) * fc, :])
    o_ref[0] = x1 + ga2_ref[0] * acc


def _post(xall, att, yf, yb, u, z, hyo, modtok, layer, lw, n_tiles):
    bsz, ntok, _ = xall.shape
    nt = ntok // TM

    def modspec(chunk):
        return pl.BlockSpec(
            (1, 1, D_MODEL),
            lambda b, i: (layer * 2 * bsz + 2 * b + jnp.where(i == nt - 1, 1, 0), 0, chunk))

    def tokspec(width):
        return pl.BlockSpec((1, TM, width), lambda b, i: (b, i, 0))

    def full(arr):
        return pl.BlockSpec(arr.shape, lambda b, i: (0,) * arr.ndim)

    weights = [lw["dskip"], lw["g_ssm"], lw["wo"], lw["g_mlp"], lw["w1"], lw["w2"]]
    return pl.pallas_call(
        _post_kernel,
        out_shape=jax.ShapeDtypeStruct((bsz, n_tiles * TM, D_MODEL), F32),
        grid=(bsz, n_tiles),
        in_specs=[tokspec(D_MODEL), tokspec(N_HEADS * D_V), tokspec(S_INNER), tokspec(S_INNER), tokspec(S_INNER),
                  tokspec(S_INNER), pl.BlockSpec((1, HY_CH, TM), lambda b, i: (b, 0, i)),
                  modspec(2), modspec(3), modspec(4), modspec(5)] + [full(w) for w in weights],
        out_specs=tokspec(D_MODEL),
        compiler_params=_cparams("parallel", "parallel"),
        name="post",
    )(xall, att, yf, yb, u, z, hyo, modtok, modtok, modtok, modtok, *weights)


def _layer_weights(i, p):
    o = np.cumsum([0, Q_LORA, KV_LORA, D_ROPE, S_INNER, S_XBC, 2 * S_HEADS, 3 * HY_CH])
    w_in = p["w_in"][i]
    wcq, wckv, wkr, wz, wx, wdt, why = (w_in[:, o[j]:o[j + 1]] for j in range(7))
    wkr_pad = jnp.zeros((D_MODEL, HEAD_PAD), F32).at[:, D_NOPE:D_QK].set(wkr)
    wuq = jnp.pad(p["w_uq"][i].reshape(Q_LORA, N_HEADS, D_QK), ((0, 0), (0, 0), (0, HEAD_PAD - D_QK)))
    wukv = p["w_ukv"][i].reshape(KV_LORA, N_HEADS, D_NOPE + D_V)
    wk = jnp.pad(wukv[:, :, :D_NOPE], ((0, 0), (0, 0), (0, HEAD_PAD - D_NOPE)))
    wv_lo = jnp.pad(wukv[:, :, D_NOPE:], ((0, 0), (0, 0), (0, HEAD_PAD - D_V)))
    wv_hi = jnp.pad(wukv[:, :, D_NOPE:], ((0, 0), (0, 0), (HEAD_PAD - D_V, 0)))
    odd = (jnp.arange(N_HEADS) % 2 == 1)[None, :, None]
    wv = jnp.where(odd, wv_hi, wv_lo)
    hw = N_HEADS * HEAD_PAD

    def headgain(g):
        return jnp.pad(g, (0, HEAD_PAD - D_QK)).reshape(1, HEAD_PAD)

    lw = dict(
        g_mix=p["g_norm_mix"][i].reshape(1, D_MODEL),
        wa=jnp.concatenate([wcq, wckv, wkr_pad], axis=1).astype(BF16),
        wz=wz.astype(BF16), wx=wx.astype(BF16),
        wdt=jnp.pad(wdt, ((0, 0), (0, LANES - 2 * S_HEADS))).astype(BF16),
        whyt=why.T.astype(BF16),
        g_cq=p["g_cq"][i].reshape(1, Q_LORA), g_ckv=p["g_ckv"][i].reshape(1, KV_LORA),
        wuq=wuq.reshape(Q_LORA, hw).astype(BF16), wk=wk.reshape(KV_LORA, hw).astype(BF16),
        wv=wv.reshape(KV_LORA, hw).astype(BF16),
        g_q=headgain(p["g_qhead"][i]), g_k=headgain(p["g_khead"][i]),
        dskip=jnp.repeat(p["d_skip_ssm"][i], S_HDIM).reshape(1, S_INNER),
        g_ssm=p["g_ssm_out"][i].reshape(1, S_INNER),
        wo=p["w_out"][i].astype(BF16), g_mlp=p["g_norm_mlp"][i].reshape(1, D_MODEL),
        w1=p["w_ff1"][i].astype(BF16), w2=p["w_ff2"][i].astype(BF16),
    )
    a = -jnp.exp(p["a_log"][i].astype(F32)).reshape(-1)
    lw["a_row"] = jnp.zeros((1, LANES), F32).at[0, :2 * S_HEADS].set(a)
    wc = p["w_conv_hy"][i]
    lw["hy"] = dict(
        wconv=jnp.zeros((3 * HY_CH, LANES), F32).at[:, :3].set(wc.T).at[:, 3].set(p["b_conv_hy"][i]),
        w1t=jnp.pad(p["w_f1"][i].T, ((0, 0), (0, (-HY_EMB) % SUBLANES))),
        b1=p["b_f1"][i].reshape(HY_HIDDEN, 1), q1=p["freq_f1"][i].reshape(HY_HIDDEN, 1),
        w2t=p["w_f2"][i].T, b2=p["b_f2"][i].reshape(HY_HIDDEN, 1), q2=p["freq_f2"][i].reshape(HY_HIDDEN, 1),
        w3f=p["w_f3"][i][:, :2 * HY_CH].T, w3b=p["w_f3"][i][:, 2 * HY_CH:].T,
        delta=jnp.tile(jnp.abs(jnp.linspace(math.log(HY_DECAY_TARGET) / HY_DECAY_PCT_LONG,
                                            math.log(HY_DECAY_TARGET) / HY_DECAY_PCT_SHORT, HY_CH, dtype=F32)),
                       2).reshape(2 * HY_CH, 1),
        d0=p["d_skip_hy"][i][0].reshape(HY_CH, 1), d1=p["d_skip_hy"][i][1].reshape(HY_CH, 1),
    )
    return lw


def _rope_tables(seq, ntok):
    pos = jnp.arange(seq)
    row = (pos // GRID_W).astype(F32)
    col = (pos % GRID_W).astype(F32)
    inv = ROPE_THETA ** (-jnp.arange(ROPE_FREQS, dtype=F32) / ROPE_FREQS)
    ang = jnp.stack([row[:, None] * inv, col[:, None] * inv], axis=1)
    cos, sin = jnp.cos(ang), jnp.sin(ang)
    ct = jnp.ones((ntok, HEAD_PAD), F32)
    sa = jnp.zeros((ntok, HEAD_PAD), F32)
    sb = jnp.zeros((ntok, HEAD_PAD), F32)
    for axis in range(2):
        lo = D_NOPE + axis * 2 * ROPE_FREQS
        ct = ct.at[:seq, lo:lo + ROPE_FREQS].set(cos[:, axis]).at[:seq, lo + ROPE_FREQS:lo + 2 * ROPE_FREQS].set(cos[:, axis])
        sa = sa.at[:seq, lo:lo + ROPE_FREQS].set(-sin[:, axis])
        sb = sb.at[:seq, lo + ROPE_FREQS:lo + 2 * ROPE_FREQS].set(sin[:, axis])
    return ct, sa, sb


def kernel(x, c, ctx, c_ctx, w_mod, b_mod, g_norm_mix, g_norm_mlp, w_in, w_out, g_cq, g_ckv, w_uq, w_ukv, g_qhead, g_khead, w_conv_ssm, b_conv_ssm, a_log, dt_bias, d_skip_ssm, g_ssm_out, w_conv_hy, b_conv_hy, w_f1, b_f1, freq_f1, w_f2, b_f2, freq_f2, w_f3, d_skip_hy, w_ff1, w_ff2):
    params = dict(w_in=w_in, w_out=w_out, g_norm_mix=g_norm_mix, g_norm_mlp=g_norm_mlp, g_cq=g_cq, g_ckv=g_ckv,
                  w_uq=w_uq, w_ukv=w_ukv, g_qhead=g_qhead, g_khead=g_khead, a_log=a_log, d_skip_ssm=d_skip_ssm,
                  g_ssm_out=g_ssm_out, w_conv_hy=w_conv_hy, b_conv_hy=b_conv_hy, w_f1=w_f1, b_f1=b_f1,
                  freq_f1=freq_f1, w_f2=w_f2, b_f2=b_f2, freq_f2=freq_f2, w_f3=w_f3, d_skip_hy=d_skip_hy,
                  w_ff1=w_ff1, w_ff2=w_ff2)
    bsz, seq, _ = x.shape
    nctx = ctx.shape[1]
    assert nctx == TM and seq % HY_BLOCK == 0 and bsz + 1 <= SUBLANES
    ntok = seq + nctx
    nlat_tiles = seq // TM

    cvec = jnp.zeros((SUBLANES, D_MODEL), F32).at[:bsz].set(c).at[bsz].set(c_ctx)
    mod = _modulation(cvec, w_mod, b_mod)
    rows = jnp.stack([jnp.arange(bsz), jnp.full((bsz,), bsz)], axis=1).reshape(-1)
    modtok = mod[:, rows, :].reshape(DEPTH * 2 * bsz, 1, 6 * D_MODEL)

    rope_tabs = _rope_tables(seq, ntok)
    tabs_lat = _dft_tables(HY_BLOCK)
    tabs_ctx = _dft_tables(nctx)
    expand = jnp.repeat(jnp.eye(LANES, 2 * S_HEADS, dtype=F32), S_HDIM, axis=1)
    exp_f, exp_b = expand[:, :S_INNER], expand[:, S_INNER:]
    gi = jnp.arange(S_GROUPS * S_STATE)[:, None] // S_STATE
    hi = jnp.arange(S_INNER)[None, :] // (S_INNER // S_GROUPS)
    blockmask = (gi == hi).astype(F32)

    xall = jnp.concatenate([x, ctx], axis=1)
    for i in range(DEPTH):
        last = i == DEPTH - 1
        lw = _layer_weights(i, params)
        q, k, v, z, xbc, dtraw, hyt = _inproj(xall, modtok, i, lw, rope_tabs)
        att = _attention(q, k, v, 0, nlat_tiles, 0, ntok, TM, 512)
        u, dt = _ssmprep(xbc, dtraw, w_conv_ssm[i], b_conv_ssm[i], dt_bias[i])
        yf = _ssd(u, dt, lw["a_row"], exp_f, blockmask, False)
        yb = _ssd(u, dt, lw["a_row"], exp_b, blockmask, True)
        hyo = _hyena(hyt, seq, 0, lw["hy"], tabs_lat, HY_BLOCK)
        if last:
            return _post(xall, att, yf, yb, u, z, hyo, modtok, i, lw, nlat_tiles)
        att_c = _attention(q, k, v, nlat_tiles, 1, seq, nctx, TM, 512)
        hyo_c = _hyena(hyt, nctx, seq // nctx, lw["hy"], tabs_ctx, nctx)
        att = jnp.concatenate([att, att_c], axis=1)
        hyo = jnp.concatenate([hyo, hyo_c], axis=2)
        xall = _post(xall, att, yf, yb, u, z, hyo, modtok, i, lw, nlat_tiles + 1)
```

```python
import functools
import math

import jax
import jax.numpy as jnp
import numpy as np
from jax import lax
from jax.experimental import pallas as pl
from jax.experimental.pallas import tpu as pltpu

F32 = jnp.float32
BF16 = jnp.bfloat16
HIGHEST = lax.Precision.HIGHEST

D_MODEL = 1024
DEPTH = 2
GRID_W = 64
EPS = 1e-6
N_HEADS = 6
D_NOPE = 64
D_ROPE = 32
D_QK = D_NOPE + D_ROPE
D_V = 64
Q_LORA = 256
KV_LORA = 128
ROPE_THETA = 10000.0
ROPE_FREQS = D_ROPE // 4
S_HEADS = 6
S_HDIM = 64
S_INNER = S_HEADS * S_HDIM
S_GROUPS = 2
S_STATE = 64
S_XBC = S_INNER + 2 * S_GROUPS * S_STATE
HY_CH = D_MODEL - N_HEADS * D_V - S_INNER
HY_BANDS = 16
HY_EMB = 1 + 2 * HY_BANDS
HY_HIDDEN = 64
HY_DECAY_PCT_SHORT = 0.3
HY_DECAY_PCT_LONG = 1.5
HY_DECAY_TARGET = 1e-2
D_FF = 4 * D_MODEL

LANES = 128
SUBLANES = 8
VMEM_LIMIT = 56 * 1024 * 1024

TM = 512
TP = 256
TQ = 512
TK = 512
ATT_UNROLL = 8
CHUNK = 128
HEAD_PAD = LANES
HY_BLOCK = 1024
HY_FT = 256


def _cparams(*sem):
    return pltpu.CompilerParams(dimension_semantics=sem, vmem_limit_bytes=VMEM_LIMIT)


def _dot(a, b, precision=None):
    return jnp.dot(a, b, preferred_element_type=F32, precision=precision)


def _dot_nt(a, b):
    return lax.dot_general(a, b, (((1,), (1,)), ((), ())), preferred_element_type=F32)


def _dot_tn(a, b):
    return lax.dot_general(a, b, (((0,), (0,)), ((), ())), preferred_element_type=F32)


def _rms(x):
    return x * lax.rsqrt(jnp.mean(x * x, axis=-1, keepdims=True) + EPS)


def _sigmoid(x):
    return 1.0 / (1.0 + jnp.exp(-x))


def _full(arr):
    return pl.BlockSpec(arr.shape, lambda *_: (0,) * arr.ndim)


def _mod_kernel(c_ref, w_ref, b_ref, o_ref):
    cv = c_ref[...]
    s = (cv * _sigmoid(cv)).astype(BF16)
    o_ref[0] = _dot(s, w_ref[0].astype(BF16)) + b_ref[0]


def _modulation(cvec, w_mod, b_mod):
    tn = 1024
    ncol = w_mod.shape[-1]
    return pl.pallas_call(
        _mod_kernel,
        out_shape=jax.ShapeDtypeStruct((DEPTH, SUBLANES, ncol), F32),
        grid=(DEPTH, ncol // tn),
        in_specs=[pl.BlockSpec((SUBLANES, D_MODEL), lambda l, j: (0, 0)),
                  pl.BlockSpec((1, D_MODEL, tn), lambda l, j: (l, 0, j)),
                  pl.BlockSpec((1, 1, tn), lambda l, j: (l, 0, j))],
        out_specs=pl.BlockSpec((1, SUBLANES, tn), lambda l, j: (l, 0, j)),
        compiler_params=_cparams("parallel", "parallel"),
        name="modulation",
    )(cvec, w_mod, b_mod.reshape(DEPTH, 1, ncol))


def _modspec(row0, chunk):
    return pl.BlockSpec((1, 1, D_MODEL), lambda b, i: (row0 + b, 0, chunk))


def _inproj_kernel(*refs, use_rope):
    (x_ref, sh_ref, sc_ref, g_ref, wa_ref, wz_ref, wx_ref, wdt_ref, whyt_ref,
     gcq_ref, gckv_ref, wuq_ref, wk_ref, wv_ref, vone_ref, gq_ref, gk_ref) = refs[:17]
    q_ref, k_ref, v_ref, z_ref, xbc_ref, dt_ref, hyt_ref = refs[-7:]
    x = x_ref[0]
    h = _rms(x) * g_ref[...]
    h = h * (1.0 + sc_ref[0]) + sh_ref[0]
    hb = h.astype(BF16)
    z_ref[0] = _dot(hb, wz_ref[...])
    xbc_ref[0] = _dot(hb, wx_ref[...])
    dt_ref[0] = _dot(hb, wdt_ref[...])
    hyt_ref[0] = _dot_nt(whyt_ref[...], hb)
    a = _dot(hb, wa_ref[...])
    cq = a[:, :Q_LORA]
    ckv = a[:, Q_LORA:Q_LORA + KV_LORA]
    krb = a[:, Q_LORA + KV_LORA:Q_LORA + KV_LORA + HEAD_PAD]
    cqn = (_rms(cq) * gcq_ref[...]).astype(BF16)
    ckvn = (_rms(ckv) * gckv_ref[...]).astype(BF16)
    qr = _dot(cqn, wuq_ref[...])
    kn = _dot(ckvn, wk_ref[...])
    v_ref[0] = (_dot(ckvn, wv_ref[...]) + vone_ref[...]).astype(BF16)
    gq = gq_ref[...]
    gk = gk_ref[...]
    if use_rope:
        wuqp_ref, gqp_ref, gkp_ref, ct_ref, sn_ref = refs[17:22]
        qp = _dot(cqn, wuqp_ref[...])
        krp = a[:, Q_LORA + KV_LORA + HEAD_PAD:]
        gqp = gqp_ref[...]
        gkp = gkp_ref[...]
        ct = ct_ref[...]
        sn = sn_ref[...]
        gq, gqp, gk, gkp = gq * ct, gqp * sn, gk * ct, gkp * sn

    def head_norm_rope(t, g, tp, gp):
        ss = jnp.sum(t * t, axis=-1, keepdims=True) * (1.0 / D_QK)
        inv = lax.rsqrt(ss + EPS)
        if not use_rope:
            return t * inv * g
        return (t * g + tp * gp) * inv

    qscale = math.log2(math.e) / math.sqrt(D_QK)
    for hh in range(N_HEADS):
        sl = slice(HEAD_PAD * hh, HEAD_PAD * (hh + 1))
        q_ref[0, :, sl] = (head_norm_rope(qr[:, sl], gq, qp[:, sl] if use_rope else None,
                                          gqp if use_rope else None) * qscale).astype(BF16)
        k_ref[0, :, sl] = head_norm_rope(kn[:, sl] + krb, gk, krp if use_rope else None,
                                         gkp if use_rope else None).astype(BF16)


def _inproj(x, modtok, mod_row0, lw, rope_tabs, tm):
    bsz, ntok, _ = x.shape
    hw = N_HEADS * HEAD_PAD

    def tok(width, dtype):
        return jax.ShapeDtypeStruct((bsz, ntok, width), dtype)

    def tokspec(width):
        return pl.BlockSpec((1, tm, width), lambda b, i: (b, i, 0))

    weights = [lw["g_mix"], lw["wa"], lw["wz"], lw["wx"], lw["wdt"], lw["whyt"], lw["g_cq"], lw["g_ckv"],
               lw["wuq"], lw["wk"], lw["wv"], lw["vone"], lw["g_q"], lw["g_k"]]
    tabs = [] if rope_tabs is None else list(rope_tabs)
    if rope_tabs is not None:
        weights += [lw["wuqp"], lw["g_qp"], lw["g_kp"]]
    tabspec = pl.BlockSpec((tm, HEAD_PAD), lambda b, i: (i, 0))
    return pl.pallas_call(
        functools.partial(_inproj_kernel, use_rope=rope_tabs is not None),
        out_shape=[tok(hw, BF16), tok(hw, BF16), tok(hw, BF16), tok(S_INNER, F32), tok(S_XBC, F32),
                   tok(LANES, F32), jax.ShapeDtypeStruct((bsz, 3 * HY_CH, ntok), F32)],
        grid=(bsz, ntok // tm),
        in_specs=[tokspec(D_MODEL), _modspec(mod_row0, 0), _modspec(mod_row0, 1)]
        + [_full(w) for w in weights] + [tabspec] * len(tabs),
        out_specs=[tokspec(hw), tokspec(hw), tokspec(hw), tokspec(S_INNER), tokspec(S_XBC), tokspec(LANES),
                   pl.BlockSpec((1, 3 * HY_CH, tm), lambda b, i: (b, 0, i))],
        compiler_params=_cparams("parallel", "parallel"),
        name="inproj",
    )(x, modtok, modtok, *weights, *tabs)


def _attn_kernel(*refs, seg_rows):
    q_ref = refs[0]
    o_ref = refs[-1]
    tq = q_ref.shape[1]
    slices = [slice(HEAD_PAD * hh, HEAD_PAD * (hh + 1)) for hh in range(2)]

    def scores(k_ref, start, size):
        return tuple(_dot_nt(q_ref[0, :, sl], k_ref[0, pl.ds(start, size), sl]) for sl in slices)

    def consume(state, s, v_ref, start, size):
        new = []
        for hh in range(2):
            m, acc = state[hh]
            m_new = jnp.maximum(m, jnp.max(s[hh], axis=-1, keepdims=True))
            p = jnp.exp2(s[hh] - m_new)
            acc = jnp.exp2(m - m_new) * acc + _dot(p.astype(BF16), v_ref[0, pl.ds(start, size), slices[hh]])
            new.append((m_new, acc))
        return tuple(new)

    state = tuple((jnp.full((tq, 1), -jnp.inf, F32), jnp.zeros((tq, HEAD_PAD), F32)) for _ in range(2))
    for seg, rows in enumerate(seg_rows):
        k_ref = refs[1 + 2 * seg]
        v_ref = refs[2 + 2 * seg]
        n_full = rows // TK
        if n_full:
            def body(t, st, k_ref=k_ref, v_ref=v_ref):
                start = pl.multiple_of(t * TK, TK)
                return consume(st, scores(k_ref, start, TK), v_ref, start, TK)

            state = lax.fori_loop(0, n_full, body, state, unroll=ATT_UNROLL)
        if rows % TK:
            state = consume(state, scores(k_ref, n_full * TK, rows % TK), v_ref, n_full * TK, rows % TK)
    acc_e = state[0][1]
    acc_o = state[1][1]
    lane = lax.broadcasted_iota(jnp.int32, (tq, HEAD_PAD), 1)
    o_ref[0] = jnp.where(lane < D_V, acc_e / acc_e[:, D_V:D_V + 1], acc_o / acc_o[:, 0:1]).astype(BF16)


def _attention(q, kvs, tq):
    bsz, nq, _ = q.shape
    pw = 2 * HEAD_PAD
    in_specs = [pl.BlockSpec((1, tq, pw), lambda b, p, i: (b, i, p))]
    args = [q]
    for k, v in kvs:
        spec = pl.BlockSpec((1, k.shape[1], pw), lambda b, p, i: (b, 0, p))
        in_specs += [spec, spec]
        args += [k, v]
    return pl.pallas_call(
        functools.partial(_attn_kernel, seg_rows=tuple(k.shape[1] for k, _ in kvs)),
        out_shape=jax.ShapeDtypeStruct((bsz, nq, N_HEADS * D_V), BF16),
        grid=(bsz, N_HEADS // 2, nq // tq),
        in_specs=in_specs,
        out_specs=pl.BlockSpec((1, tq, 2 * D_V), lambda b, p, i: (b, i, p)),
        compiler_params=_cparams("parallel", "parallel", "parallel"),
        name="attention",
    )(*args)


def _ssmprep_kernel(x_ref, prev_ref, next_ref, dtr_ref, w_ref, b_ref, dtb_ref, u_ref, dt_ref):
    i = pl.program_id(1)
    x = x_ref[0]
    tp = x.shape[0]
    pv = prev_ref[0][SUBLANES - 1:SUBLANES, :] * jnp.where(i == 0, 0.0, 1.0)
    nx = next_ref[0][0:1, :] * jnp.where(i == pl.num_programs(1) - 1, 0.0, 1.0)
    rows = lax.broadcasted_iota(jnp.int32, x.shape, 0)
    up = jnp.where(rows == 0, pv, pltpu.roll(x, 1, 0))
    dn = jnp.where(rows == tp - 1, nx, pltpu.roll(x, tp - 1, 0))
    w = w_ref[...]
    y = up * w[0:1] + x * w[1:2] + dn * w[2:3] + b_ref[...]
    u_ref[0] = y * _sigmoid(y)
    t = dtr_ref[0] + dtb_ref[...]
    dt_ref[0] = jnp.maximum(t, 0.0) + jnp.log(1.0 + jnp.exp(-jnp.abs(t)))


def _ssmprep(xbc, dtraw, wpad, bias, dtb):
    bsz, ntok, width = xbc.shape
    r = TP // SUBLANES
    nblk8 = ntok // SUBLANES
    return pl.pallas_call(
        _ssmprep_kernel,
        out_shape=[jax.ShapeDtypeStruct(xbc.shape, F32), jax.ShapeDtypeStruct(dtraw.shape, F32)],
        grid=(bsz, ntok // TP),
        in_specs=[pl.BlockSpec((1, TP, width), lambda b, i: (b, i, 0)),
                  pl.BlockSpec((1, SUBLANES, width), lambda b, i: (b, jnp.maximum(i * r - 1, 0), 0)),
                  pl.BlockSpec((1, SUBLANES, width), lambda b, i: (b, jnp.minimum((i + 1) * r, nblk8 - 1), 0)),
                  pl.BlockSpec((1, TP, LANES), lambda b, i: (b, i, 0)),
                  _full(wpad), _full(bias), _full(dtb)],
        out_specs=[pl.BlockSpec((1, TP, width), lambda b, i: (b, i, 0)),
                   pl.BlockSpec((1, TP, LANES), lambda b, i: (b, i, 0))],
        compiler_params=_cparams("parallel", "parallel"),
        name="ssm_prep",
    )(xbc, xbc, xbc, dtraw, wpad, bias, dtb)


def _ssd_kernel(u_ref, dt_ref, a_ref, e_ref, bm_ref, init_ref, y_ref, fin_ref, st_ref, *, reverse):
    @pl.when(pl.program_id(1) == 0)
    def _():
        st_ref[...] = init_ref[0]

    u = u_ref[0]
    dt = dt_ref[0]
    xs = u[:, :S_INNER]
    bmat = u[:, S_INNER:S_INNER + LANES]
    cmat = u[:, S_INNER + LANES:]
    off = S_HEADS if reverse else 0
    dta = dt * a_ref[...]
    ii = lax.broadcasted_iota(jnp.int32, (CHUNK, CHUNK), 0)
    jj = lax.broadcasted_iota(jnp.int32, (CHUNK, CHUNK), 1)
    tri = (jj >= ii) if reverse else (jj <= ii)
    cum = _dot(tri.astype(F32), dta, HIGHEST)
    cum_t = cum.T
    expand = e_ref[...]
    cum_e = _dot(cum, expand, HIGHEST)
    dt_e = _dot(dt, expand, HIGHEST)
    tot_e = cum_e[0:1] if reverse else cum_e[CHUNK - 1:CHUNK]
    xdt = xs * dt_e
    st = st_ref[...]
    lane = lax.broadcasted_iota(jnp.int32, (CHUNK, LANES), 1)
    y_off = _dot(cmat.astype(BF16), st.astype(BF16)) * jnp.exp(cum_e)
    new = _dot(bmat.T.astype(BF16), (xdt * jnp.exp(tot_e - cum_e)).astype(BF16))
    st_new = (st * jnp.exp(tot_e) + new) * bm_ref[...]
    st_ref[...] = st_new

    @pl.when(pl.program_id(1) == pl.num_programs(1) - 1)
    def _():
        fin_ref[0] = st_new

    b16 = bmat.astype(BF16)
    cbs = [_dot_nt(jnp.where((lane // S_STATE) == g, cmat, 0.0).astype(BF16), b16) for g in range(S_GROUPS)]
    for pair in range(S_HEADS // 2):
        xp = xdt[:, LANES * pair:LANES * (pair + 1)]
        acc = None
        for half in range(2):
            hh = 2 * pair + half
            seg = cum[:, off + hh:off + hh + 1] - cum_t[off + hh:off + hh + 1, :]
            dec = jnp.where(tri, jnp.exp(jnp.where(tri, seg, 0.0)), 0.0)
            sc = (cbs[hh // (S_HEADS // S_GROUPS)] * dec).astype(BF16)
            xh = jnp.where((lane // S_HDIM) == half, xp, 0.0).astype(BF16)
            part = _dot(sc, xh)
            acc = part if acc is None else acc + part
        sl = slice(LANES * pair, LANES * (pair + 1))
        y_ref[0, :, sl] = acc + y_off[:, sl]


def _ssd(u, dt, a_row, expand, blockmask, init, reverse):
    bsz, ntok, width = u.shape
    nchunk = ntok // CHUNK

    def cidx(s):
        return nchunk - 1 - s if reverse else s

    srows = S_GROUPS * S_STATE
    return pl.pallas_call(
        functools.partial(_ssd_kernel, reverse=reverse),
        out_shape=[jax.ShapeDtypeStruct((bsz, ntok, S_INNER), F32),
                   jax.ShapeDtypeStruct((bsz, srows, S_INNER), F32)],
        grid=(bsz, nchunk),
        in_specs=[pl.BlockSpec((1, CHUNK, width), lambda b, s: (b, cidx(s), 0)),
                  pl.BlockSpec((1, CHUNK, LANES), lambda b, s: (b, cidx(s), 0)),
                  _full(a_row), _full(expand), _full(blockmask),
                  pl.BlockSpec((1, srows, S_INNER), lambda b, s: (b, 0, 0))],
        out_specs=[pl.BlockSpec((1, CHUNK, S_INNER), lambda b, s: (b, cidx(s), 0)),
                   pl.BlockSpec((1, srows, S_INNER), lambda b, s: (b, 0, 0))],
        scratch_shapes=[pltpu.VMEM((srows, S_INNER), F32)],
        compiler_params=_cparams("parallel", "arbitrary"),
        name="ssd_bwd" if reverse else "ssd_fwd",
    )(u, dt, a_row, expand, blockmask, init)


def _hyprep_kernel(x_ref, w_ref, o_ref):
    x = x_ref[0]
    n = x.shape[1]
    t = lax.broadcasted_iota(jnp.int32, x.shape, 1)
    up = jnp.where(t == 0, 0.0, pltpu.roll(x, 1, 1))
    dn = jnp.where(t == n - 1, 0.0, pltpu.roll(x, n - 1, 1))
    w = w_ref[...]
    o_ref[0] = up * w[:, 0:1] + x * w[:, 1:2] + dn * w[:, 2:3] + w[:, 3:4]


def _hyprep(hyt, wcols):
    bsz, rows, n = hyt.shape
    rt = 256
    return pl.pallas_call(
        _hyprep_kernel,
        out_shape=jax.ShapeDtypeStruct((bsz, rows, n), F32),
        grid=(bsz, rows // rt),
        in_specs=[pl.BlockSpec((1, rt, n), lambda b, r: (b, r, 0)),
                  pl.BlockSpec((rt, LANES), lambda b, r: (r, 0))],
        out_specs=pl.BlockSpec((1, rt, n), lambda b, r: (b, r, 0)),
        compiler_params=_cparams("parallel", "parallel"),
        name="hyena_prep",
    )(hyt, wcols)


def _hyfilt_kernel(ft_ref, w1_ref, b1_ref, q1_ref, w2_ref, b2_ref, q2_ref, w3b_ref, w3f_ref, dl_ref, o_ref, h_ref):
    n = ft_ref.shape[1]
    half = n // 2

    @pl.when(pl.program_id(0) == 0)
    def _():
        h = jnp.sin(q1_ref[...] * (_dot(w1_ref[...], ft_ref[...], HIGHEST) + b1_ref[...]))
        h_ref[...] = jnp.sin(q2_ref[...] * (_dot(w2_ref[...], h, HIGHEST) + b2_ref[...]))

    fb = _dot(w3b_ref[...], h_ref[:, :half], HIGHEST)
    ff = _dot(w3f_ref[...], h_ref[:, half:], HIGHEST)
    f = jnp.concatenate([fb, ff], axis=1)
    f = f * jnp.exp(-ft_ref[0:1, :] * dl_ref[...])
    pos = lax.broadcasted_iota(jnp.int32, f.shape, 1)
    f = jnp.where(pos == 0, 0.0, f)
    o_ref[...] = f / (jnp.sum(jnp.abs(f), axis=1, keepdims=True) + EPS)


def _hyfilt(feats_t, hw):
    n = feats_t.shape[1]
    rows = 2 * HY_CH
    rt = 128
    args = [feats_t, hw["w1t"], hw["b1"], hw["q1"], hw["w2t"], hw["b2"], hw["q2"]]
    return pl.pallas_call(
        _hyfilt_kernel,
        out_shape=jax.ShapeDtypeStruct((rows, n), F32),
        grid=(rows // rt,),
        in_specs=[_full(a) for a in args]
        + [pl.BlockSpec((rt, HY_HIDDEN), lambda r: (r, 0)), pl.BlockSpec((rt, HY_HIDDEN), lambda r: (r, 0)),
           pl.BlockSpec((rt, 1), lambda r: (r, 0))],
        out_specs=pl.BlockSpec((rt, n), lambda r: (r, 0)),
        scratch_shapes=[pltpu.VMEM((HY_HIDDEN, n), F32)],
        compiler_params=_cparams("arbitrary"),
        name="hyena_filter",
    )(*args, hw["w3b"], hw["w3f"], hw["delta"])


def _hyspec_kernel(f_ref, c_ref, s_ref, g_ref, *, blk, nb):
    kt = pl.program_id(0)
    ftile = c_ref.shape[1]
    ctab = c_ref[...].astype(BF16)
    stab = s_ref[...].astype(BF16)
    freq = lax.broadcasted_iota(jnp.int32, (2 * HY_CH, ftile), 1) + kt * ftile
    sigma = jnp.where((freq & 1) == 0, 1.0, -1.0)
    scale = 2.0 / (2 * blk)
    prev = None
    for e in range(2 * nb):
        phi = f_ref[:, e * blk:(e + 1) * blk]
        p16 = phi.astype(BF16)
        a = _dot(p16, ctab)
        bs = _dot(p16, stab)
        cur = (a, bs, phi[:, 0:1])
        if prev is not None:
            gr = (a + sigma * prev[1]) * scale
            gi = (sigma * (prev[0] - prev[2]) - bs) * scale
            for o in range(2):
                g_ref[o, e - 1, 0] = gr[o * HY_CH:(o + 1) * HY_CH]
                g_ref[o, e - 1, 1] = gi[o * HY_CH:(o + 1) * HY_CH]
        prev = cur


def _hyspec(filt, ctab, stab, blk):
    n = filt.shape[1]
    nb = n // (2 * blk)
    ft = min(HY_FT, blk)
    return pl.pallas_call(
        functools.partial(_hyspec_kernel, blk=blk, nb=nb),
        out_shape=jax.ShapeDtypeStruct((2, 2 * nb - 1, 2, HY_CH, blk), F32),
        grid=(blk // ft,),
        in_specs=[_full(filt),
                  pl.BlockSpec((blk, ft), lambda k: (0, k)),
                  pl.BlockSpec((blk, ft), lambda k: (0, k))],
        out_specs=pl.BlockSpec((2, 2 * nb - 1, 2, HY_CH, ft), lambda k: (0, 0, 0, 0, k)),
        compiler_params=_cparams("parallel"),
        name="hyena_spectra",
    )(filt, ctab, stab)


def _hyconv_kernel(u_ref, m_ref, d_ref, g_ref, c_ref, s_ref, o_ref, ub_ref, acc_ref, *, blk, nb):
    kt = pl.program_id(1)

    @pl.when(kt == 0)
    def _():
        ub_ref[...] = u_ref[0, 0].astype(BF16)
        acc_ref[...] = jnp.zeros_like(acc_ref)

    ctab = c_ref[...].astype(BF16)
    stab = s_ref[...].astype(BF16)
    xr = []
    xs = []
    for j in range(nb):
        uj = ub_ref[:, j * blk:(j + 1) * blk]
        xr.append(_dot(uj, ctab))
        xs.append(_dot(uj, stab))
    for i in range(nb):
        yr = None
        ys = None
        for j in range(nb):
            d = i - j + nb - 1
            gr = g_ref[0, d, 0]
            gi = g_ref[0, d, 1]
            tr = gr * xr[j] + gi * xs[j]
            ts = gr * xs[j] - gi * xr[j]
            yr = tr if yr is None else yr + tr
            ys = ts if ys is None else ys + ts
        acc_ref[:, i * blk:(i + 1) * blk] += _dot_nt(yr.astype(BF16), ctab) + _dot_nt(ys.astype(BF16), stab)

    @pl.when(kt == pl.num_programs(1) - 1)
    def _():
        o_ref[0] = (m_ref[0, 0] * (acc_ref[...] + u_ref[0, 0] * d_ref[...])).astype(o_ref.dtype)


def _hyconv(u4, usel, m4, msel, dcol, gspec, order, tabs, blk, out_dtype):
    bsz, _, _, n = u4.shape
    nb = n // blk
    ft = min(HY_FT, blk)
    ctab, stab = tabs
    return pl.pallas_call(
        functools.partial(_hyconv_kernel, blk=blk, nb=nb),
        out_shape=jax.ShapeDtypeStruct((bsz, HY_CH, n), out_dtype),
        grid=(bsz, blk // ft),
        in_specs=[pl.BlockSpec((1, 1, HY_CH, n), lambda b, k: (b, usel, 0, 0)),
                  pl.BlockSpec((1, 1, HY_CH, n), lambda b, k: (b, msel, 0, 0)),
                  pl.BlockSpec((HY_CH, 1), lambda b, k: (0, 0)),
                  pl.BlockSpec((1, 2 * nb - 1, 2, HY_CH, ft), lambda b, k: (order, 0, 0, 0, k)),
                  pl.BlockSpec((blk, ft), lambda b, k: (0, k)),
                  pl.BlockSpec((blk, ft), lambda b, k: (0, k))],
        out_specs=pl.BlockSpec((1, HY_CH, n), lambda b, k: (b, 0, 0)),
        scratch_shapes=[pltpu.VMEM((HY_CH, n), BF16), pltpu.VMEM((HY_CH, n), F32)],
        compiler_params=_cparams("parallel", "arbitrary"),
        name="hyena_conv",
    )(u4, m4, dcol, gspec, ctab, stab)


def _dft_tables(blk):
    s = np.arange(blk, dtype=np.int64)[:, None]
    k = np.arange(blk, dtype=np.int64)[None, :]
    ang = ((s * (2 * k + 1)) % (4 * blk)).astype(np.float64) * (2.0 * math.pi / (4 * blk))
    return jnp.asarray(np.cos(ang), F32), jnp.asarray(np.sin(ang), F32)


def _hyena_features(length):
    p = np.arange(length)
    tb = np.where(p == 0, 0, length - p)
    t = np.concatenate([tb, p]).astype(np.float64)
    t01 = t / (length - 1)
    w = (2.0 * math.pi / length) * t
    bands = np.linspace(1e-4, HY_BANDS - 1, HY_BANDS)[:, None]
    feats = np.concatenate([t01[None, :], np.cos(bands * w[None, :]), -np.sin(bands * w[None, :])], axis=0)
    pad = (-feats.shape[0]) % SUBLANES
    return jnp.asarray(np.pad(feats, ((0, pad), (0, 0))), F32)


def _hyena(hyt, hw, tabs, blk, out_dtype):
    bsz, _, n = hyt.shape
    p = _hyprep(hyt, hw["wconv"])
    filt = _hyfilt(_hyena_features(n), hw)
    gspec = _hyspec(filt, tabs[0], tabs[1], blk)
    p4 = p.reshape(bsz, 3, HY_CH, n)
    z1 = _hyconv(p4, 0, p4, 1, hw["d0"], gspec, 0, tabs, blk, F32)
    return _hyconv(z1.reshape(bsz, 1, HY_CH, n), 0, p4, 2, hw["d1"], gspec, 1, tabs, blk, out_dtype)


def _post_kernel(x_ref, att_ref, yf_ref, yb_ref, xs_ref, z_ref, hyo_ref, ga1_ref, sh2_ref, sc2_ref, ga2_ref,
                 dsk_ref, gss_ref, wo_ref, g2_ref, w1_ref, w2_ref, o_ref):
    x = x_ref[0]
    y = yf_ref[0] + yb_ref[0] + xs_ref[0] * dsk_ref[...]
    zz = z_ref[0]
    y = y * (zz * _sigmoid(zz))
    gw = S_INNER // S_GROUPS
    lane = lax.broadcasted_iota(jnp.int32, y.shape, 1)
    first = lane < gw
    y2 = y * y
    s0 = jnp.sum(jnp.where(first, y2, 0.0), axis=-1, keepdims=True)
    s1 = jnp.sum(y2, axis=-1, keepdims=True) - s0
    inv = jnp.where(first, lax.rsqrt(s0 * (1.0 / gw) + EPS), lax.rsqrt(s1 * (1.0 / gw) + EPS))
    ssm = (y * inv * gss_ref[...]).astype(BF16)
    na = N_HEADS * D_V
    mix = _dot(att_ref[0], wo_ref[0:na]) + _dot(ssm, wo_ref[na:na + S_INNER])
    mix = mix + _dot_tn(hyo_ref[0], wo_ref[na + S_INNER:])
    x1 = x + ga1_ref[0] * mix
    h2 = (_rms(x1) * g2_ref[...]) * (1.0 + sc2_ref[0]) + sh2_ref[0]
    hb = h2.astype(BF16)
    acc = jnp.zeros_like(x1)
    fc = 1024
    for c in range(D_FF // fc):
        t = jnp.maximum(_dot(hb, w1_ref[:, c * fc:(c + 1) * fc]), 0.0)
        acc = acc + _dot((t * t).astype(BF16), w2_ref[c * fc:(c + 1) * fc, :])
    o_ref[0] = x1 + ga2_ref[0] * acc


def _post(x, att, yf, yb, u, z, hyo, modtok, mod_row0, lw, tm):
    bsz, ntok, _ = x.shape

    def tokspec(width):
        return pl.BlockSpec((1, tm, width), lambda b, i: (b, i, 0))

    weights = [lw["dskip"], lw["g_ssm"], lw["wo"], lw["g_mlp"], lw["w1"], lw["w2"]]
    return pl.pallas_call(
        _post_kernel,
        out_shape=jax.ShapeDtypeStruct((bsz, ntok, D_MODEL), F32),
        grid=(bsz, ntok // tm),
        in_specs=[tokspec(D_MODEL), tokspec(N_HEADS * D_V), tokspec(S_INNER), tokspec(S_INNER), tokspec(S_INNER),
                  tokspec(S_INNER), pl.BlockSpec((1, HY_CH, tm), lambda b, i: (b, 0, i)),
                  _modspec(mod_row0, 2), _modspec(mod_row0, 3), _modspec(mod_row0, 4), _modspec(mod_row0, 5)]
        + [_full(w) for w in weights],
        out_specs=tokspec(D_MODEL),
        compiler_params=_cparams("parallel", "parallel"),
        name="post",
    )(x, att, yf, yb, u, z, hyo, modtok, modtok, modtok, modtok, *weights)


def _layer_weights(i, p):
    o = np.cumsum([0, Q_LORA, KV_LORA, D_ROPE, S_INNER, S_XBC, 2 * S_HEADS, 3 * HY_CH])
    w_in = p["w_in"][i]
    wcq, wckv, wkr, wz, wx, wdt, why = (w_in[:, o[j]:o[j + 1]] for j in range(7))
    ropepad = ((0, 0), (D_NOPE, HEAD_PAD - D_QK))
    partner = np.arange(D_ROPE) ^ ROPE_FREQS
    wkr_pad = jnp.concatenate([jnp.pad(wkr, ropepad), jnp.pad(wkr[:, partner], ropepad)], axis=1)
    wuq3 = p["w_uq"][i].reshape(Q_LORA, N_HEADS, D_QK)
    wuq = jnp.pad(wuq3, ((0, 0), (0, 0), (0, HEAD_PAD - D_QK)))
    wuqp = jnp.pad(wuq3[:, :, D_NOPE:][:, :, partner], ((0, 0), (0, 0), (D_NOPE, HEAD_PAD - D_QK)))
    wukv = p["w_ukv"][i].reshape(KV_LORA, N_HEADS, D_NOPE + D_V)
    wk = jnp.pad(wukv[:, :, :D_NOPE], ((0, 0), (0, 0), (0, HEAD_PAD - D_NOPE)))
    wv_lo = jnp.pad(wukv[:, :, D_NOPE:], ((0, 0), (0, 0), (0, HEAD_PAD - D_V)))
    wv_hi = jnp.pad(wukv[:, :, D_NOPE:], ((0, 0), (0, 0), (HEAD_PAD - D_V, 0)))
    odd = (np.arange(N_HEADS) % 2 == 1)[None, :, None]
    wv = jnp.where(odd, wv_hi, wv_lo)
    hw = N_HEADS * HEAD_PAD
    vone = np.zeros((N_HEADS, HEAD_PAD), np.float32)
    vone[0::2, D_V] = 1.0
    vone[1::2, 0] = 1.0

    def headgain(g):
        return jnp.pad(g, (0, HEAD_PAD - D_QK)).reshape(1, HEAD_PAD)

    def partnergain(g):
        return jnp.pad(g[D_NOPE:][partner], (D_NOPE, HEAD_PAD - D_QK)).reshape(1, HEAD_PAD)

    lw = dict(
        g_mix=p["g_norm_mix"][i].reshape(1, D_MODEL),
        wa=jnp.concatenate([wcq, wckv, wkr_pad], axis=1).astype(BF16),
        wz=wz.astype(BF16), wx=wx.astype(BF16),
        wdt=jnp.pad(wdt, ((0, 0), (0, LANES - 2 * S_HEADS))).astype(BF16),
        whyt=why.T.astype(BF16),
        g_cq=p["g_cq"][i].reshape(1, Q_LORA), g_ckv=p["g_ckv"][i].reshape(1, KV_LORA),
        wuq=wuq.reshape(Q_LORA, hw).astype(BF16), wk=wk.reshape(KV_LORA, hw).astype(BF16),
        wv=wv.reshape(KV_LORA, hw).astype(BF16), vone=jnp.asarray(vone.reshape(1, hw)),
        g_q=headgain(p["g_qhead"][i]), g_k=headgain(p["g_khead"][i]),
        wuqp=wuqp.reshape(Q_LORA, hw).astype(BF16),
        g_qp=partnergain(p["g_qhead"][i]), g_kp=partnergain(p["g_khead"][i]),
        dskip=jnp.repeat(p["d_skip_ssm"][i], S_HDIM).reshape(1, S_INNER),
        g_ssm=p["g_ssm_out"][i].reshape(1, S_INNER),
        wo=p["w_out"][i].astype(BF16), g_mlp=p["g_norm_mlp"][i].reshape(1, D_MODEL),
        w1=p["w_ff1"][i].astype(BF16), w2=p["w_ff2"][i].astype(BF16),
    )
    a = -jnp.exp(p["a_log"][i].astype(F32)).reshape(1, -1)
    lw["a_row"] = jnp.pad(a, ((0, 0), (0, LANES - 2 * S_HEADS)))
    width = p["w_conv_ssm"].shape[-1]
    lw["ssm_w"] = jnp.pad(p["w_conv_ssm"][i], ((0, SUBLANES - 3), (0, 0)))
    lw["ssm_b"] = p["b_conv_ssm"][i].reshape(1, width)
    lw["dt_b"] = jnp.pad(p["dt_bias"][i].reshape(1, -1), ((0, 0), (0, LANES - 2 * S_HEADS)))
    taps = jnp.concatenate([p["w_conv_hy"][i].T, p["b_conv_hy"][i][:, None]], axis=1)
    lw["hy"] = dict(
        wconv=jnp.pad(taps, ((0, 0), (0, LANES - 4))),
        w1t=jnp.pad(p["w_f1"][i].T, ((0, 0), (0, (-HY_EMB) % SUBLANES))),
        b1=p["b_f1"][i].reshape(HY_HIDDEN, 1), q1=p["freq_f1"][i].reshape(HY_HIDDEN, 1),
        w2t=p["w_f2"][i].T, b2=p["b_f2"][i].reshape(HY_HIDDEN, 1), q2=p["freq_f2"][i].reshape(HY_HIDDEN, 1),
        w3f=p["w_f3"][i][:, :2 * HY_CH].T, w3b=p["w_f3"][i][:, 2 * HY_CH:].T,
        delta=jnp.asarray(np.tile(np.abs(np.linspace(math.log(HY_DECAY_TARGET) / HY_DECAY_PCT_LONG,
                                                     math.log(HY_DECAY_TARGET) / HY_DECAY_PCT_SHORT, HY_CH)),
                                  2).reshape(2 * HY_CH, 1), F32),
        d0=p["d_skip_hy"][i][0].reshape(HY_CH, 1), d1=p["d_skip_hy"][i][1].reshape(HY_CH, 1),
    )
    return lw


def _rope_tables(seq):
    pos = np.arange(seq)
    inv = ROPE_THETA ** (-np.arange(ROPE_FREQS, dtype=np.float64) / ROPE_FREQS)
    ang = np.stack([(pos // GRID_W)[:, None] * inv, (pos % GRID_W)[:, None] * inv], axis=1)
    cos, sin = np.cos(ang), np.sin(ang)
    ct = np.ones((seq, HEAD_PAD))
    sn = np.zeros((seq, HEAD_PAD))
    for axis in range(2):
        lo = D_NOPE + axis * 2 * ROPE_FREQS
        mid = lo + ROPE_FREQS
        ct[:, lo:mid] = cos[:, axis]
        ct[:, mid:mid + ROPE_FREQS] = cos[:, axis]
        sn[:, lo:mid] = -sin[:, axis]
        sn[:, mid:mid + ROPE_FREQS] = sin[:, axis]
    return jnp.asarray(ct, F32), jnp.asarray(sn, F32)


def kernel(x, c, ctx, c_ctx, w_mod, b_mod, g_norm_mix, g_norm_mlp, w_in, w_out, g_cq, g_ckv, w_uq, w_ukv, g_qhead, g_khead, w_conv_ssm, b_conv_ssm, a_log, dt_bias, d_skip_ssm, g_ssm_out, w_conv_hy, b_conv_hy, w_f1, b_f1, freq_f1, w_f2, b_f2, freq_f2, w_f3, d_skip_hy, w_ff1, w_ff2):
    params = dict(w_in=w_in, w_out=w_out, g_norm_mix=g_norm_mix, g_norm_mlp=g_norm_mlp, g_cq=g_cq, g_ckv=g_ckv,
                  w_uq=w_uq, w_ukv=w_ukv, g_qhead=g_qhead, g_khead=g_khead, a_log=a_log, d_skip_ssm=d_skip_ssm,
                  g_ssm_out=g_ssm_out, w_conv_hy=w_conv_hy, b_conv_hy=b_conv_hy, w_f1=w_f1, b_f1=b_f1,
                  freq_f1=freq_f1, w_f2=w_f2, b_f2=b_f2, freq_f2=freq_f2, w_f3=w_f3, d_skip_hy=d_skip_hy,
                  w_ff1=w_ff1, w_ff2=w_ff2, w_conv_ssm=w_conv_ssm, b_conv_ssm=b_conv_ssm, dt_bias=dt_bias)
    bsz, seq, _ = x.shape
    nctx = ctx.shape[1]
    assert seq % HY_BLOCK == 0 and seq % TM == 0 and nctx % TP == 0 and bsz + 1 <= SUBLANES

    cvec = jnp.zeros((SUBLANES, D_MODEL), F32).at[:bsz].set(c).at[bsz].set(c_ctx)
    mod = _modulation(cvec, w_mod, b_mod)
    modtok = mod.reshape(DEPTH * SUBLANES, 1, 6 * D_MODEL)

    rope_tabs = _rope_tables(seq)
    tabs_lat = _dft_tables(HY_BLOCK)
    tabs_ctx = _dft_tables(nctx)
    expand = np.repeat(np.eye(LANES, 2 * S_HEADS, dtype=np.float32), S_HDIM, axis=1)
    exp_f, exp_b = jnp.asarray(expand[:, :S_INNER]), jnp.asarray(expand[:, S_INNER:])
    gi = np.arange(S_GROUPS * S_STATE)[:, None] // S_STATE
    hi = np.arange(S_INNER)[None, :] // (S_INNER // S_GROUPS)
    blockmask = jnp.asarray((gi == hi).astype(np.float32))
    zero_state = jnp.zeros((bsz, S_GROUPS * S_STATE, S_INNER), F32)

    xl = x
    xc = ctx.reshape(1, bsz * nctx, D_MODEL)
    for i in range(DEPTH):
        last = i == DEPTH - 1
        lw = _layer_weights(i, params)
        row_l = i * SUBLANES
        row_c = i * SUBLANES + bsz

        q_c, k_c, v_c, z_c, xbc_c, dtr_c, hyt_c = _inproj(xc, modtok, row_c, lw, None, TM)
        per_b = lambda t: t.reshape(bsz, nctx, t.shape[-1])
        k_c, v_c = per_b(k_c), per_b(v_c)
        u_c, dt_c = _ssmprep(per_b(xbc_c), per_b(dtr_c), lw["ssm_w"], lw["ssm_b"], lw["dt_b"])
        yf_c, s_fwd = _ssd(u_c, dt_c, lw["a_row"], exp_f, blockmask, zero_state, False)
        yb_c, s_bwd = _ssd(u_c, dt_c, lw["a_row"], exp_b, blockmask, zero_state, True)

        q, k, v, z, xbc, dtraw, hyt = _inproj(xl, modtok, row_l, lw, rope_tabs, TM)
        att = _attention(q, [(k, v), (k_c, v_c)], TQ)
        u, dt = _ssmprep(xbc, dtraw, lw["ssm_w"], lw["ssm_b"], lw["dt_b"])
        yf, _ = _ssd(u, dt, lw["a_row"], exp_f, blockmask, s_fwd, False)
        yb, _ = _ssd(u, dt, lw["a_row"], exp_b, blockmask, s_bwd, True)
        hyo = _hyena(hyt, lw["hy"], tabs_lat, HY_BLOCK, BF16)
        xl = _post(xl, att, yf, yb, u, z, hyo, modtok, row_l, lw, TM)
        if last:
            return xl

        att_c = _attention(per_b(q_c), [(k_c, v_c)], nctx)
        hyt_cb = hyt_c.reshape(3 * HY_CH, bsz, nctx).transpose(1, 0, 2)
        hyo_c = _hyena(hyt_cb, lw["hy"], tabs_ctx, nctx, BF16)
        flat = lambda t: t.reshape(1, bsz * nctx, t.shape[-1])
        hyo_cf = hyo_c.transpose(1, 0, 2).reshape(1, HY_CH, bsz * nctx)
        xc = _post(xc, flat(att_c), flat(yf_c), flat(yb_c), flat(u_c), z_c, hyo_cf, modtok, row_c, lw, TM)
```

```python
import functools
import math

import jax
import jax.numpy as jnp
import numpy as np
from jax import lax
from jax.experimental import pallas as pl
from jax.experimental.pallas import tpu as pltpu

F32 = jnp.float32
BF16 = jnp.bfloat16
HIGHEST = lax.Precision.HIGHEST

D_MODEL = 1024
DEPTH = 2
GRID_W = 64
EPS = 1e-6
N_HEADS = 6
D_NOPE = 64
D_ROPE = 32
D_QK = D_NOPE + D_ROPE
D_V = 64
Q_LORA = 256
KV_LORA = 128
ROPE_THETA = 10000.0
ROPE_FREQS = D_ROPE // 4
S_HEADS = 6
S_HDIM = 64
S_INNER = S_HEADS * S_HDIM
S_GROUPS = 2
S_STATE = 64
S_XBC = S_INNER + 2 * S_GROUPS * S_STATE
HY_CH = D_MODEL - N_HEADS * D_V - S_INNER
HY_BANDS = 16
HY_EMB = 1 + 2 * HY_BANDS
HY_HIDDEN = 64
HY_DECAY_PCT_SHORT = 0.3
HY_DECAY_PCT_LONG = 1.5
HY_DECAY_TARGET = 1e-2
D_FF = 4 * D_MODEL

LANES = 128
SUBLANES = 8
VMEM_LIMIT = 56 * 1024 * 1024

TM = 512
HALO = 16
N_INPROJ_IN = 22
SSD_GROUP = 4
TQ = 512
TK = 512
ATT_UNROLL = 8
CHUNK = 128
HEAD_PAD = LANES
HY_BLOCK = 1024
HY_FT = 256


def _cparams(*sem):
    return pltpu.CompilerParams(dimension_semantics=sem, vmem_limit_bytes=VMEM_LIMIT)


def _dot(a, b, precision=None):
    return jnp.dot(a, b, preferred_element_type=F32, precision=precision)


def _dot_nt(a, b):
    return lax.dot_general(a, b, (((1,), (1,)), ((), ())), preferred_element_type=F32)


def _dot_tn(a, b):
    return lax.dot_general(a, b, (((0,), (0,)), ((), ())), preferred_element_type=F32)


def _rms(x):
    return x * lax.rsqrt(jnp.mean(x * x, axis=-1, keepdims=True) + EPS)


def _sigmoid(x):
    return 1.0 / (1.0 + jnp.exp(-x))


def _full(arr):
    return pl.BlockSpec(arr.shape, lambda *_: (0,) * arr.ndim)


def _mod_kernel(c_ref, w_ref, b_ref, o_ref):
    cv = c_ref[...]
    s = (cv * _sigmoid(cv)).astype(BF16)
    o_ref[0] = _dot(s, w_ref[0].astype(BF16)) + b_ref[0]


def _modulation(cvec, w_mod, b_mod):
    tn = 1024
    ncol = w_mod.shape[-1]
    return pl.pallas_call(
        _mod_kernel,
        out_shape=jax.ShapeDtypeStruct((DEPTH, SUBLANES, ncol), F32),
        grid=(DEPTH, ncol // tn),
        in_specs=[pl.BlockSpec((SUBLANES, D_MODEL), lambda l, j: (0, 0)),
                  pl.BlockSpec((1, D_MODEL, tn), lambda l, j: (l, 0, j)),
                  pl.BlockSpec((1, 1, tn), lambda l, j: (l, 0, j))],
        out_specs=pl.BlockSpec((1, SUBLANES, tn), lambda l, j: (l, 0, j)),
        compiler_params=_cparams("parallel", "parallel"),
        name="modulation",
    )(cvec, w_mod, b_mod.reshape(DEPTH, 1, ncol))


def _modspec(row0, chunk):
    return pl.BlockSpec((1, 1, D_MODEL), lambda b, i: (row0 + b, 0, chunk))


def _inproj_kernel(*refs, use_rope, seq_len):
    (x_ref, xp_ref, xn_ref, sh_ref, sc_ref, g_ref, wa_ref, wz_ref, wx_ref, wdt_ref, why_ref, cx_ref, ch_ref,
     dtb_ref, gcq_ref, gckv_ref, wuq_ref, wk_ref, wv_ref, vone_ref, gq_ref, gk_ref) = refs[:N_INPROJ_IN]
    q_ref, k_ref, v_ref, z_ref, u_ref, dt_ref, hyt_ref = refs[-7:]
    tm = x_ref.shape[1]

    def normmod(xv):
        return ((_rms(xv) * g_ref[...]) * (1.0 + sc_ref[0]) + sh_ref[0]).astype(BF16)

    hb = normmod(x_ref[0])
    hb_ext = jnp.concatenate([normmod(xp_ref[0]), hb, normmod(xn_ref[0])], axis=0)
    pos = (pl.program_id(1) * tm + lax.broadcasted_iota(jnp.int32, (tm, 1), 0)) % seq_len
    has_prev = pos != 0
    has_next = pos != seq_len - 1

    def conv3(w_ref, taps_ref):
        ext = _dot(hb_ext, w_ref[...])
        n = ext.shape[0]
        up = pltpu.roll(ext, 1, 0)[HALO:HALO + tm]
        dn = pltpu.roll(ext, n - 1, 0)[HALO:HALO + tm]
        taps = taps_ref[...]
        return (jnp.where(has_prev, up, 0.0) * taps[0:1] + ext[HALO:HALO + tm] * taps[1:2]
                + jnp.where(has_next, dn, 0.0) * taps[2:3] + taps[3:4])

    z_ref[0] = _dot(hb, wz_ref[...])
    xc = conv3(wx_ref, cx_ref)
    u_ref[0] = xc * _sigmoid(xc)
    t = _dot(hb, wdt_ref[...]) + dtb_ref[...]
    dt_ref[0] = jnp.maximum(t, 0.0) + jnp.log(1.0 + jnp.exp(-jnp.abs(t)))
    hyt_ref[0] = conv3(why_ref, ch_ref).T
    a = _dot(hb, wa_ref[...])
    cq = a[:, :Q_LORA]
    ckv = a[:, Q_LORA:Q_LORA + KV_LORA]
    krb = a[:, Q_LORA + KV_LORA:Q_LORA + KV_LORA + HEAD_PAD]
    cqn = (_rms(cq) * gcq_ref[...]).astype(BF16)
    ckvn = (_rms(ckv) * gckv_ref[...]).astype(BF16)
    qr = _dot(cqn, wuq_ref[...])
    kn = _dot(ckvn, wk_ref[...])
    v_ref[0] = (_dot(ckvn, wv_ref[...]) + vone_ref[...]).astype(BF16)
    gq = gq_ref[...]
    gk = gk_ref[...]
    if use_rope:
        wuqp_ref, gqp_ref, gkp_ref, ct_ref, sn_ref = refs[N_INPROJ_IN:N_INPROJ_IN + 5]
        qp = _dot(cqn, wuqp_ref[...])
        krp = a[:, Q_LORA + KV_LORA + HEAD_PAD:]
        gqp = gqp_ref[...]
        gkp = gkp_ref[...]
        ct = ct_ref[...]
        sn = sn_ref[...]
        gq, gqp, gk, gkp = gq * ct, gqp * sn, gk * ct, gkp * sn

    def head_norm_rope(t, g, tp, gp):
        ss = jnp.sum(t * t, axis=-1, keepdims=True) * (1.0 / D_QK)
        inv = lax.rsqrt(ss + EPS)
        if not use_rope:
            return t * inv * g
        return (t * g + tp * gp) * inv

    qscale = math.log2(math.e) / math.sqrt(D_QK)
    for hh in range(N_HEADS):
        sl = slice(HEAD_PAD * hh, HEAD_PAD * (hh + 1))
        q_ref[0, :, sl] = (head_norm_rope(qr[:, sl], gq, qp[:, sl] if use_rope else None,
                                          gqp if use_rope else None) * qscale).astype(BF16)
        k_ref[0, :, sl] = head_norm_rope(kn[:, sl] + krb, gk, krp if use_rope else None,
                                         gkp if use_rope else None).astype(BF16)


def _inproj(x, modtok, mod_row0, lw, rope_tabs, tm, seq_len):
    bsz, ntok, _ = x.shape
    hw = N_HEADS * HEAD_PAD
    r = tm // HALO
    nhalo = ntok // HALO

    def tok(width, dtype):
        return jax.ShapeDtypeStruct((bsz, ntok, width), dtype)

    def tokspec(width):
        return pl.BlockSpec((1, tm, width), lambda b, i: (b, i, 0))

    weights = [lw["g_mix"], lw["wa"], lw["wz"], lw["wx"], lw["wdt"], lw["why"], lw["taps_x"], lw["taps_hy"],
               lw["dt_b"], lw["g_cq"], lw["g_ckv"], lw["wuq"], lw["wk"], lw["wv"], lw["vone"], lw["g_q"], lw["g_k"]]
    assert 5 + len(weights) == N_INPROJ_IN
    tabs = [] if rope_tabs is None else list(rope_tabs)
    if rope_tabs is not None:
        weights += [lw["wuqp"], lw["g_qp"], lw["g_kp"]]
    tabspec = pl.BlockSpec((tm, HEAD_PAD), lambda b, i: (i, 0))
    return pl.pallas_call(
        functools.partial(_inproj_kernel, use_rope=rope_tabs is not None, seq_len=seq_len),
        out_shape=[tok(hw, BF16), tok(hw, BF16), tok(hw, BF16), tok(S_INNER, F32), tok(S_XBC, F32),
                   tok(LANES, F32), jax.ShapeDtypeStruct((bsz, 3 * HY_CH, ntok), F32)],
        grid=(bsz, ntok // tm),
        in_specs=[tokspec(D_MODEL),
                  pl.BlockSpec((1, HALO, D_MODEL), lambda b, i: (b, jnp.maximum(i * r - 1, 0), 0)),
                  pl.BlockSpec((1, HALO, D_MODEL), lambda b, i: (b, jnp.minimum((i + 1) * r, nhalo - 1), 0)),
                  _modspec(mod_row0, 0), _modspec(mod_row0, 1)]
        + [_full(w) for w in weights] + [tabspec] * len(tabs),
        out_specs=[tokspec(hw), tokspec(hw), tokspec(hw), tokspec(S_INNER), tokspec(S_XBC), tokspec(LANES),
                   pl.BlockSpec((1, 3 * HY_CH, tm), lambda b, i: (b, 0, i))],
        compiler_params=_cparams("parallel", "parallel"),
        name="inproj",
    )(x, x, x, modtok, modtok, *weights, *tabs)


def _attn_kernel(*refs, seg_rows):
    q_ref = refs[0]
    o_ref = refs[-1]
    tq = q_ref.shape[1]
    slices = [slice(HEAD_PAD * hh, HEAD_PAD * (hh + 1)) for hh in range(2)]

    def scores(k_ref, start, size):
        return tuple(_dot_nt(q_ref[0, :, sl], k_ref[0, pl.ds(start, size), sl]) for sl in slices)

    def consume(state, s, v_ref, start, size):
        new = []
        for hh in range(2):
            m, acc = state[hh]
            m_new = jnp.maximum(m, jnp.max(s[hh], axis=-1, keepdims=True))
            p = jnp.exp2(s[hh] - m_new)
            acc = jnp.exp2(m - m_new) * acc + _dot(p.astype(BF16), v_ref[0, pl.ds(start, size), slices[hh]])
            new.append((m_new, acc))
        return tuple(new)

    state = tuple((jnp.full((tq, 1), -jnp.inf, F32), jnp.zeros((tq, HEAD_PAD), F32)) for _ in range(2))
    for seg, rows in enumerate(seg_rows):
        k_ref = refs[1 + 2 * seg]
        v_ref = refs[2 + 2 * seg]
        n_full = rows // TK
        if n_full:
            def body(t, st, k_ref=k_ref, v_ref=v_ref):
                start = pl.multiple_of(t * TK, TK)
                return consume(st, scores(k_ref, start, TK), v_ref, start, TK)

            state = lax.fori_loop(0, n_full, body, state, unroll=ATT_UNROLL)
        if rows % TK:
            state = consume(state, scores(k_ref, n_full * TK, rows % TK), v_ref, n_full * TK, rows % TK)
    acc_e = state[0][1]
    acc_o = state[1][1]
    lane = lax.broadcasted_iota(jnp.int32, (tq, HEAD_PAD), 1)
    o_ref[0] = jnp.where(lane < D_V, acc_e / acc_e[:, D_V:D_V + 1], acc_o / acc_o[:, 0:1]).astype(BF16)


def _attention(q, kvs, tq):
    bsz, nq, _ = q.shape
    pw = 2 * HEAD_PAD
    in_specs = [pl.BlockSpec((1, tq, pw), lambda b, p, i: (b, i, p))]
    args = [q]
    for k, v in kvs:
        spec = pl.BlockSpec((1, k.shape[1], pw), lambda b, p, i: (b, 0, p))
        in_specs += [spec, spec]
        args += [k, v]
    return pl.pallas_call(
        functools.partial(_attn_kernel, seg_rows=tuple(k.shape[1] for k, _ in kvs)),
        out_shape=jax.ShapeDtypeStruct((bsz, nq, N_HEADS * D_V), BF16),
        grid=(bsz, N_HEADS // 2, nq // tq),
        in_specs=in_specs,
        out_specs=pl.BlockSpec((1, tq, 2 * D_V), lambda b, p, i: (b, i, p)),
        compiler_params=_cparams("parallel", "parallel", "parallel"),
        name="attention",
    )(*args)


def _split3(x):
    hi = x.astype(BF16)
    r1 = x - hi.astype(F32)
    mid = r1.astype(BF16)
    return hi, mid, (r1 - mid.astype(F32)).astype(BF16)


def _ssd_chunk(u, dt, a_row, st, reverse):
    off = S_HEADS if reverse else 0
    hpg = S_HEADS // S_GROUPS
    xs = u[:, :S_INNER]
    bmat = u[:, S_INNER:S_INNER + LANES]
    cmat = u[:, S_INNER + LANES:]
    ii = lax.broadcasted_iota(jnp.int32, (CHUNK, CHUNK), 0)
    jj = lax.broadcasted_iota(jnp.int32, (CHUNK, CHUNK), 1)
    tri = (jj >= ii) if reverse else (jj <= ii)
    tri16 = tri.astype(BF16)
    cum = sum(_dot(tri16, part) for part in _split3(dt * a_row))
    cum_t = cum.T
    dt_t = dt.T
    tot = cum[0:1] if reverse else cum[CHUNK - 1:CHUNK]
    lane = lax.broadcasted_iota(jnp.int32, (CHUNK, LANES), 1)
    lane2 = lax.broadcasted_iota(jnp.int32, (2 * CHUNK, LANES), 1)
    b16 = bmat.astype(BF16)
    bmat_t = bmat.T
    cms = [jnp.where((lane // S_STATE) == g, cmat, 0.0) for g in range(S_GROUPS)]
    cbs = [_dot_nt(cms[g].astype(BF16), b16) for g in range(S_GROUPS)]
    bts = [jnp.where((ii // S_STATE) == g, bmat_t, 0.0) for g in range(S_GROUPS)]
    ys = []
    sts = []
    for pair in range(S_HEADS // 2):
        sl = slice(LANES * pair, LANES * (pair + 1))
        both = jnp.concatenate([xs[:, sl], st[:, sl]], axis=0)
        acc = None
        new = None
        decay = []
        for half in range(2):
            hh = 2 * pair + half
            g = hh // hpg
            c = off + hh
            col = cum[:, c:c + 1]
            row = cum_t[c:c + 1, :]
            dtr = dt_t[c:c + 1, :]
            dec = jnp.where(tri, jnp.exp(jnp.where(tri, col - row, 0.0)), 0.0)
            lhs = jnp.concatenate([cbs[g] * dec * dtr, cms[g] * jnp.exp(col)], axis=1).astype(BF16)
            rhs = jnp.where((lane2 // S_HDIM) == half, both, 0.0).astype(BF16)
            part = _dot(lhs, rhs)
            acc = part if acc is None else acc + part
            tot_h = tot[:, c:c + 1]
            wrow = jnp.exp(tot_h - row) * dtr
            pn = _dot((bts[g] * wrow).astype(BF16), rhs[:CHUNK])
            new = pn if new is None else new + pn
            decay.append(jnp.exp(tot_h))
        ys.append(acc)
        sts.append(st[:, sl] * jnp.where(lane[0:1] < S_HDIM, decay[0], decay[1]) + new)
    return jnp.concatenate(ys, axis=1), jnp.concatenate(sts, axis=1)


def _ssd_kernel(uf_ref, dtf_ref, ub_ref, dtb_ref, a_ref, initf_ref, initb_ref,
                yf_ref, yb_ref, finf_ref, finb_ref, stf_ref, stb_ref, *, nchunks):
    @pl.when(pl.program_id(1) == 0)
    def _():
        stf_ref[...] = initf_ref[0]
        stb_ref[...] = initb_ref[0]

    a_row = a_ref[...]
    st = stf_ref[...]
    for k in range(nchunks):
        rows = slice(k * CHUNK, (k + 1) * CHUNK)
        y, st = _ssd_chunk(uf_ref[0, rows, :], dtf_ref[0, rows, :], a_row, st, False)
        yf_ref[0, rows, :] = y
    stf_ref[...] = st
    stf_last = st
    st = stb_ref[...]
    for k in reversed(range(nchunks)):
        rows = slice(k * CHUNK, (k + 1) * CHUNK)
        y, st = _ssd_chunk(ub_ref[0, rows, :], dtb_ref[0, rows, :], a_row, st, True)
        yb_ref[0, rows, :] = y
    stb_ref[...] = st

    @pl.when(pl.program_id(1) == pl.num_programs(1) - 1)
    def _():
        finf_ref[0] = stf_last
        finb_ref[0] = st


def _ssd(u, dt, a_row, init_f, init_b, nchunks):
    bsz, ntok, width = u.shape
    blk = nchunks * CHUNK
    nsteps = ntok // blk
    srows = S_GROUPS * S_STATE
    fwd = lambda b, s: (b, s, 0)
    bwd = lambda b, s: (b, nsteps - 1 - s, 0)
    state = pl.BlockSpec((1, srows, S_INNER), lambda b, s: (b, 0, 0))
    yshape = jax.ShapeDtypeStruct((bsz, ntok, S_INNER), F32)
    sshape = jax.ShapeDtypeStruct((bsz, srows, S_INNER), F32)
    return pl.pallas_call(
        functools.partial(_ssd_kernel, nchunks=nchunks),
        out_shape=[yshape, yshape, sshape, sshape],
        grid=(bsz, nsteps),
        in_specs=[pl.BlockSpec((1, blk, width), fwd), pl.BlockSpec((1, blk, LANES), fwd),
                  pl.BlockSpec((1, blk, width), bwd), pl.BlockSpec((1, blk, LANES), bwd),
                  _full(a_row), state, state],
        out_specs=[pl.BlockSpec((1, blk, S_INNER), fwd), pl.BlockSpec((1, blk, S_INNER), bwd), state, state],
        scratch_shapes=[pltpu.VMEM((srows, S_INNER), F32), pltpu.VMEM((srows, S_INNER), F32)],
        compiler_params=_cparams("parallel", "arbitrary"),
        name="ssd",
    )(u, dt, u, dt, a_row, init_f, init_b)


def _hyfilt_kernel(ft_ref, w1_ref, b1_ref, q1_ref, w2_ref, b2_ref, q2_ref, w3b_ref, w3f_ref, dl_ref, o_ref, h_ref):
    n = ft_ref.shape[1]
    half = n // 2

    @pl.when(pl.program_id(0) == 0)
    def _():
        h = jnp.sin(q1_ref[...] * (_dot(w1_ref[...], ft_ref[...], HIGHEST) + b1_ref[...]))
        h_ref[...] = jnp.sin(q2_ref[...] * (_dot(w2_ref[...], h, HIGHEST) + b2_ref[...]))

    fb = _dot(w3b_ref[...], h_ref[:, :half], HIGHEST)
    ff = _dot(w3f_ref[...], h_ref[:, half:], HIGHEST)
    f = jnp.concatenate([fb, ff], axis=1)
    f = f * jnp.exp(-ft_ref[0:1, :] * dl_ref[...])
    pos = lax.broadcasted_iota(jnp.int32, f.shape, 1)
    f = jnp.where(pos == 0, 0.0, f)
    o_ref[...] = f / (jnp.sum(jnp.abs(f), axis=1, keepdims=True) + EPS)


def _hyfilt(feats_t, hw):
    n = feats_t.shape[1]
    rows = 2 * HY_CH
    rt = 128
    args = [feats_t, hw["w1t"], hw["b1"], hw["q1"], hw["w2t"], hw["b2"], hw["q2"]]
    return pl.pallas_call(
        _hyfilt_kernel,
        out_shape=jax.ShapeDtypeStruct((rows, n), F32),
        grid=(rows // rt,),
        in_specs=[_full(a) for a in args]
        + [pl.BlockSpec((rt, HY_HIDDEN), lambda r: (r, 0)), pl.BlockSpec((rt, HY_HIDDEN), lambda r: (r, 0)),
           pl.BlockSpec((rt, 1), lambda r: (r, 0))],
        out_specs=pl.BlockSpec((rt, n), lambda r: (r, 0)),
        scratch_shapes=[pltpu.VMEM((HY_HIDDEN, n), F32)],
        compiler_params=_cparams("arbitrary"),
        name="hyena_filter",
    )(*args, hw["w3b"], hw["w3f"], hw["delta"])


def _hyspec_kernel(f_ref, c_ref, s_ref, g_ref, *, blk, nb):
    kt = pl.program_id(0)
    ftile = c_ref.shape[1]
    ctab = c_ref[...].astype(BF16)
    stab = s_ref[...].astype(BF16)
    freq = lax.broadcasted_iota(jnp.int32, (2 * HY_CH, ftile), 1) + kt * ftile
    sigma = jnp.where((freq & 1) == 0, 1.0, -1.0)
    scale = 2.0 / (2 * blk)
    prev = None
    for e in range(2 * nb):
        phi = f_ref[:, e * blk:(e + 1) * blk]
        p16 = phi.astype(BF16)
        a = _dot(p16, ctab)
        bs = _dot(p16, stab)
        cur = (a, bs, phi[:, 0:1])
        if prev is not None:
            gr = (a + sigma * prev[1]) * scale
            gi = (sigma * (prev[0] - prev[2]) - bs) * scale
            for o in range(2):
                g_ref[o, e - 1, 0] = gr[o * HY_CH:(o + 1) * HY_CH]
                g_ref[o, e - 1, 1] = gi[o * HY_CH:(o + 1) * HY_CH]
        prev = cur


def _hyspec(filt, ctab, stab, blk):
    n = filt.shape[1]
    nb = n // (2 * blk)
    ft = min(HY_FT, blk)
    return pl.pallas_call(
        functools.partial(_hyspec_kernel, blk=blk, nb=nb),
        out_shape=jax.ShapeDtypeStruct((2, 2 * nb - 1, 2, HY_CH, blk), F32),
        grid=(blk // ft,),
        in_specs=[_full(filt),
                  pl.BlockSpec((blk, ft), lambda k: (0, k)),
                  pl.BlockSpec((blk, ft), lambda k: (0, k))],
        out_specs=pl.BlockSpec((2, 2 * nb - 1, 2, HY_CH, ft), lambda k: (0, 0, 0, 0, k)),
        compiler_params=_cparams("parallel"),
        name="hyena_spectra",
    )(filt, ctab, stab)


def _hyconv_kernel(u_ref, m_ref, d_ref, g_ref, c_ref, s_ref, o_ref, ub_ref, acc_ref, *, blk, nb):
    kt = pl.program_id(1)

    @pl.when(kt == 0)
    def _():
        ub_ref[...] = u_ref[0, 0].astype(BF16)
        acc_ref[...] = jnp.zeros_like(acc_ref)

    ctab = c_ref[...].astype(BF16)
    stab = s_ref[...].astype(BF16)
    xr = []
    xs = []
    for j in range(nb):
        uj = ub_ref[:, j * blk:(j + 1) * blk]
        xr.append(_dot(uj, ctab))
        xs.append(_dot(uj, stab))
    for i in range(nb):
        yr = None
        ys = None
        for j in range(nb):
            d = i - j + nb - 1
            gr = g_ref[0, d, 0]
            gi = g_ref[0, d, 1]
            tr = gr * xr[j] + gi * xs[j]
            ts = gr * xs[j] - gi * xr[j]
            yr = tr if yr is None else yr + tr
            ys = ts if ys is None else ys + ts
        acc_ref[:, i * blk:(i + 1) * blk] += _dot_nt(yr.astype(BF16), ctab) + _dot_nt(ys.astype(BF16), stab)

    @pl.when(kt == pl.num_programs(1) - 1)
    def _():
        o_ref[0] = (m_ref[0, 0] * (acc_ref[...] + u_ref[0, 0] * d_ref[...])).astype(o_ref.dtype)


def _hyconv(u4, usel, m4, msel, dcol, gspec, order, tabs, blk, out_dtype):
    bsz, _, _, n = u4.shape
    nb = n // blk
    ft = min(HY_FT, blk)
    ctab, stab = tabs
    return pl.pallas_call(
        functools.partial(_hyconv_kernel, blk=blk, nb=nb),
        out_shape=jax.ShapeDtypeStruct((bsz, HY_CH, n), out_dtype),
        grid=(bsz, blk // ft),
        in_specs=[pl.BlockSpec((1, 1, HY_CH, n), lambda b, k: (b, usel, 0, 0)),
                  pl.BlockSpec((1, 1, HY_CH, n), lambda b, k: (b, msel, 0, 0)),
                  pl.BlockSpec((HY_CH, 1), lambda b, k: (0, 0)),
                  pl.BlockSpec((1, 2 * nb - 1, 2, HY_CH, ft), lambda b, k: (order, 0, 0, 0, k)),
                  pl.BlockSpec((blk, ft), lambda b, k: (0, k)),
                  pl.BlockSpec((blk, ft), lambda b, k: (0, k))],
        out_specs=pl.BlockSpec((1, HY_CH, n), lambda b, k: (b, 0, 0)),
        scratch_shapes=[pltpu.VMEM((HY_CH, n), BF16), pltpu.VMEM((HY_CH, n), F32)],
        compiler_params=_cparams("parallel", "arbitrary"),
        name="hyena_conv",
    )(u4, m4, dcol, gspec, ctab, stab)


def _dft_tables(blk):
    s = np.arange(blk, dtype=np.int64)[:, None]
    k = np.arange(blk, dtype=np.int64)[None, :]
    ang = ((s * (2 * k + 1)) % (4 * blk)).astype(np.float64) * (2.0 * math.pi / (4 * blk))
    return jnp.asarray(np.cos(ang), F32), jnp.asarray(np.sin(ang), F32)


def _hyena_features(length):
    p = np.arange(length)
    tb = np.where(p == 0, 0, length - p)
    t = np.concatenate([tb, p]).astype(np.float64)
    t01 = t / (length - 1)
    w = (2.0 * math.pi / length) * t
    bands = np.linspace(1e-4, HY_BANDS - 1, HY_BANDS)[:, None]
    feats = np.concatenate([t01[None, :], np.cos(bands * w[None, :]), -np.sin(bands * w[None, :])], axis=0)
    pad = (-feats.shape[0]) % SUBLANES
    return jnp.asarray(np.pad(feats, ((0, pad), (0, 0))), F32)


def _hyena(p, hw, tabs, blk, out_dtype):
    bsz, _, n = p.shape
    filt = _hyfilt(_hyena_features(n), hw)
    gspec = _hyspec(filt, tabs[0], tabs[1], blk)
    p4 = p.reshape(bsz, 3, HY_CH, n)
    z1 = _hyconv(p4, 0, p4, 1, hw["d0"], gspec, 0, tabs, blk, F32)
    return _hyconv(z1.reshape(bsz, 1, HY_CH, n), 0, p4, 2, hw["d1"], gspec, 1, tabs, blk, out_dtype)


def _post_kernel(x_ref, att_ref, yf_ref, yb_ref, xs_ref, z_ref, hyo_ref, ga1_ref, sh2_ref, sc2_ref, ga2_ref,
                 dsk_ref, gss_ref, wo_ref, g2_ref, w1_ref, w2_ref, o_ref):
    x = x_ref[0]
    y = yf_ref[0] + yb_ref[0] + xs_ref[0] * dsk_ref[...]
    zz = z_ref[0]
    y = y * (zz * _sigmoid(zz))
    gw = S_INNER // S_GROUPS
    lane = lax.broadcasted_iota(jnp.int32, y.shape, 1)
    first = lane < gw
    y2 = y * y
    s0 = jnp.sum(jnp.where(first, y2, 0.0), axis=-1, keepdims=True)
    s1 = jnp.sum(y2, axis=-1, keepdims=True) - s0
    inv = jnp.where(first, lax.rsqrt(s0 * (1.0 / gw) + EPS), lax.rsqrt(s1 * (1.0 / gw) + EPS))
    ssm = (y * inv * gss_ref[...]).astype(BF16)
    na = N_HEADS * D_V
    mix = _dot(att_ref[0], wo_ref[0:na]) + _dot(ssm, wo_ref[na:na + S_INNER])
    mix = mix + _dot_tn(hyo_ref[0], wo_ref[na + S_INNER:])
    x1 = x + ga1_ref[0] * mix
    h2 = (_rms(x1) * g2_ref[...]) * (1.0 + sc2_ref[0]) + sh2_ref[0]
    hb = h2.astype(BF16)
    acc = jnp.zeros_like(x1)
    fc = 1024
    for c in range(D_FF // fc):
        t = jnp.maximum(_dot(hb, w1_ref[:, c * fc:(c + 1) * fc]), 0.0)
        acc = acc + _dot((t * t).astype(BF16), w2_ref[c * fc:(c + 1) * fc, :])
    o_ref[0] = x1 + ga2_ref[0] * acc


def _post(x, att, yf, yb, u, z, hyo, modtok, mod_row0, lw, tm):
    bsz, ntok, _ = x.shape

    def tokspec(width):
        return pl.BlockSpec((1, tm, width), lambda b, i: (b, i, 0))

    weights = [lw["dskip"], lw["g_ssm"], lw["wo"], lw["g_mlp"], lw["w1"], lw["w2"]]
    return pl.pallas_call(
        _post_kernel,
        out_shape=jax.ShapeDtypeStruct((bsz, ntok, D_MODEL), F32),
        grid=(bsz, ntok // tm),
        in_specs=[tokspec(D_MODEL), tokspec(N_HEADS * D_V), tokspec(S_INNER), tokspec(S_INNER), tokspec(S_INNER),
                  tokspec(S_INNER), pl.BlockSpec((1, HY_CH, tm), lambda b, i: (b, 0, i)),
                  _modspec(mod_row0, 2), _modspec(mod_row0, 3), _modspec(mod_row0, 4), _modspec(mod_row0, 5)]
        + [_full(w) for w in weights],
        out_specs=tokspec(D_MODEL),
        compiler_params=_cparams("parallel", "parallel"),
        name="post",
    )(x, att, yf, yb, u, z, hyo, modtok, modtok, modtok, modtok, *weights)


def _layer_weights(i, p):
    o = np.cumsum([0, Q_LORA, KV_LORA, D_ROPE, S_INNER, S_XBC, 2 * S_HEADS, 3 * HY_CH])
    w_in = p["w_in"][i]
    wcq, wckv, wkr, wz, wx, wdt, why = (w_in[:, o[j]:o[j + 1]] for j in range(7))
    ropepad = ((0, 0), (D_NOPE, HEAD_PAD - D_QK))
    partner = np.arange(D_ROPE) ^ ROPE_FREQS
    wkr_pad = jnp.concatenate([jnp.pad(wkr, ropepad), jnp.pad(wkr[:, partner], ropepad)], axis=1)
    wuq3 = p["w_uq"][i].reshape(Q_LORA, N_HEADS, D_QK)
    wuq = jnp.pad(wuq3, ((0, 0), (0, 0), (0, HEAD_PAD - D_QK)))
    wuqp = jnp.pad(wuq3[:, :, D_NOPE:][:, :, partner], ((0, 0), (0, 0), (D_NOPE, HEAD_PAD - D_QK)))
    wukv = p["w_ukv"][i].reshape(KV_LORA, N_HEADS, D_NOPE + D_V)
    wk = jnp.pad(wukv[:, :, :D_NOPE], ((0, 0), (0, 0), (0, HEAD_PAD - D_NOPE)))
    wv_lo = jnp.pad(wukv[:, :, D_NOPE:], ((0, 0), (0, 0), (0, HEAD_PAD - D_V)))
    wv_hi = jnp.pad(wukv[:, :, D_NOPE:], ((0, 0), (0, 0), (HEAD_PAD - D_V, 0)))
    odd = (np.arange(N_HEADS) % 2 == 1)[None, :, None]
    wv = jnp.where(odd, wv_hi, wv_lo)
    hw = N_HEADS * HEAD_PAD
    vone = np.zeros((N_HEADS, HEAD_PAD), np.float32)
    vone[0::2, D_V] = 1.0
    vone[1::2, 0] = 1.0

    def headgain(g):
        return jnp.pad(g, (0, HEAD_PAD - D_QK)).reshape(1, HEAD_PAD)

    def partnergain(g):
        return jnp.pad(g[D_NOPE:][partner], (D_NOPE, HEAD_PAD - D_QK)).reshape(1, HEAD_PAD)

    lw = dict(
        g_mix=p["g_norm_mix"][i].reshape(1, D_MODEL),
        wa=jnp.concatenate([wcq, wckv, wkr_pad], axis=1).astype(BF16),
        wz=wz.astype(BF16), wx=wx.astype(BF16),
        wdt=jnp.pad(wdt, ((0, 0), (0, LANES - 2 * S_HEADS))).astype(BF16),
        why=why.astype(BF16),
        g_cq=p["g_cq"][i].reshape(1, Q_LORA), g_ckv=p["g_ckv"][i].reshape(1, KV_LORA),
        wuq=wuq.reshape(Q_LORA, hw).astype(BF16), wk=wk.reshape(KV_LORA, hw).astype(BF16),
        wv=wv.reshape(KV_LORA, hw).astype(BF16), vone=jnp.asarray(vone.reshape(1, hw)),
        g_q=headgain(p["g_qhead"][i]), g_k=headgain(p["g_khead"][i]),
        wuqp=wuqp.reshape(Q_LORA, hw).astype(BF16),
        g_qp=partnergain(p["g_qhead"][i]), g_kp=partnergain(p["g_khead"][i]),
        dskip=jnp.repeat(p["d_skip_ssm"][i], S_HDIM).reshape(1, S_INNER),
        g_ssm=p["g_ssm_out"][i].reshape(1, S_INNER),
        wo=p["w_out"][i].astype(BF16), g_mlp=p["g_norm_mlp"][i].reshape(1, D_MODEL),
        w1=p["w_ff1"][i].astype(BF16), w2=p["w_ff2"][i].astype(BF16),
    )
    a = -jnp.exp(p["a_log"][i].astype(F32)).reshape(1, -1)
    lw["a_row"] = jnp.pad(a, ((0, 0), (0, LANES - 2 * S_HEADS)))
    lw["taps_x"] = jnp.pad(jnp.concatenate([p["w_conv_ssm"][i], p["b_conv_ssm"][i][None, :]], axis=0),
                           ((0, SUBLANES - 4), (0, 0)))
    lw["taps_hy"] = jnp.pad(jnp.concatenate([p["w_conv_hy"][i], p["b_conv_hy"][i][None, :]], axis=0),
                            ((0, SUBLANES - 4), (0, 0)))
    lw["dt_b"] = jnp.pad(p["dt_bias"][i].reshape(1, -1), ((0, 0), (0, LANES - 2 * S_HEADS)))
    lw["hy"] = dict(
        w1t=jnp.pad(p["w_f1"][i].T, ((0, 0), (0, (-HY_EMB) % SUBLANES))),
        b1=p["b_f1"][i].reshape(HY_HIDDEN, 1), q1=p["freq_f1"][i].reshape(HY_HIDDEN, 1),
        w2t=p["w_f2"][i].T, b2=p["b_f2"][i].reshape(HY_HIDDEN, 1), q2=p["freq_f2"][i].reshape(HY_HIDDEN, 1),
        w3f=p["w_f3"][i][:, :2 * HY_CH].T, w3b=p["w_f3"][i][:, 2 * HY_CH:].T,
        delta=jnp.asarray(np.tile(np.abs(np.linspace(math.log(HY_DECAY_TARGET) / HY_DECAY_PCT_LONG,
                                                     math.log(HY_DECAY_TARGET) / HY_DECAY_PCT_SHORT, HY_CH)),
                                  2).reshape(2 * HY_CH, 1), F32),
        d0=p["d_skip_hy"][i][0].reshape(HY_CH, 1), d1=p["d_skip_hy"][i][1].reshape(HY_CH, 1),
    )
    return lw


def _rope_tables(seq):
    pos = np.arange(seq)
    inv = ROPE_THETA ** (-np.arange(ROPE_FREQS, dtype=np.float64) / ROPE_FREQS)
    ang = np.stack([(pos // GRID_W)[:, None] * inv, (pos % GRID_W)[:, None] * inv], axis=1)
    cos, sin = np.cos(ang), np.sin(ang)
    ct = np.ones((seq, HEAD_PAD))
    sn = np.zeros((seq, HEAD_PAD))
    for axis in range(2):
        lo = D_NOPE + axis * 2 * ROPE_FREQS
        mid = lo + ROPE_FREQS
        ct[:, lo:mid] = cos[:, axis]
        ct[:, mid:mid + ROPE_FREQS] = cos[:, axis]
        sn[:, lo:mid] = -sin[:, axis]
        sn[:, mid:mid + ROPE_FREQS] = sin[:, axis]
    return jnp.asarray(ct, F32), jnp.asarray(sn, F32)


def kernel(x, c, ctx, c_ctx, w_mod, b_mod, g_norm_mix, g_norm_mlp, w_in, w_out, g_cq, g_ckv, w_uq, w_ukv, g_qhead, g_khead, w_conv_ssm, b_conv_ssm, a_log, dt_bias, d_skip_ssm, g_ssm_out, w_conv_hy, b_conv_hy, w_f1, b_f1, freq_f1, w_f2, b_f2, freq_f2, w_f3, d_skip_hy, w_ff1, w_ff2):
    params = dict(w_in=w_in, w_out=w_out, g_norm_mix=g_norm_mix, g_norm_mlp=g_norm_mlp, g_cq=g_cq, g_ckv=g_ckv,
                  w_uq=w_uq, w_ukv=w_ukv, g_qhead=g_qhead, g_khead=g_khead, a_log=a_log, d_skip_ssm=d_skip_ssm,
                  g_ssm_out=g_ssm_out, w_conv_hy=w_conv_hy, b_conv_hy=b_conv_hy, w_f1=w_f1, b_f1=b_f1,
                  freq_f1=freq_f1, w_f2=w_f2, b_f2=b_f2, freq_f2=freq_f2, w_f3=w_f3, d_skip_hy=d_skip_hy,
                  w_ff1=w_ff1, w_ff2=w_ff2, w_conv_ssm=w_conv_ssm, b_conv_ssm=b_conv_ssm, dt_bias=dt_bias)
    bsz, seq, _ = x.shape
    nctx = ctx.shape[1]
    assert seq % HY_BLOCK == 0 and seq % TM == 0 and (bsz * nctx) % TM == 0 and TM % nctx == 0
    assert bsz + 1 <= SUBLANES and CHUNK == S_GROUPS * S_STATE and nctx % CHUNK == 0

    cvec = jnp.zeros((SUBLANES, D_MODEL), F32).at[:bsz].set(c).at[bsz].set(c_ctx)
    mod = _modulation(cvec, w_mod, b_mod)
    modtok = mod.reshape(DEPTH * SUBLANES, 1, 6 * D_MODEL)

    rope_tabs = _rope_tables(seq)
    tabs_lat = _dft_tables(HY_BLOCK)
    tabs_ctx = _dft_tables(nctx)
    zero_state = jnp.zeros((bsz, S_GROUPS * S_STATE, S_INNER), F32)

    xl = x
    xc = ctx.reshape(1, bsz * nctx, D_MODEL)
    for i in range(DEPTH):
        last = i == DEPTH - 1
        lw = _layer_weights(i, params)
        row_l = i * SUBLANES
        row_c = i * SUBLANES + bsz

        q_c, k_c, v_c, z_c, u_c, dt_c, hyt_c = _inproj(xc, modtok, row_c, lw, None, TM, nctx)
        per_b = lambda t: t.reshape(bsz, nctx, t.shape[-1])
        k_c, v_c, u_c, dt_c = per_b(k_c), per_b(v_c), per_b(u_c), per_b(dt_c)
        yf_c, yb_c, s_fwd, s_bwd = _ssd(u_c, dt_c, lw["a_row"], zero_state, zero_state, nctx // CHUNK)

        q, k, v, z, u, dt, hyt = _inproj(xl, modtok, row_l, lw, rope_tabs, TM, seq)
        att = _attention(q, [(k, v), (k_c, v_c)], TQ)
        yf, yb, _, _ = _ssd(u, dt, lw["a_row"], s_fwd, s_bwd, SSD_GROUP)
        hyo = _hyena(hyt, lw["hy"], tabs_lat, HY_BLOCK, BF16)
        xl = _post(xl, att, yf, yb, u, z, hyo, modtok, row_l, lw, TM)
        if last:
            return xl

        att_c = _attention(per_b(q_c), [(k_c, v_c)], nctx)
        hyt_cb = hyt_c.reshape(3 * HY_CH, bsz, nctx).transpose(1, 0, 2)
        hyo_c = _hyena(hyt_cb, lw["hy"], tabs_ctx, nctx, BF16)
        flat = lambda t: t.reshape(1, bsz * nctx, t.shape[-1])
        hyo_cf = hyo_c.transpose(1, 0, 2).reshape(1, HY_CH, bsz * nctx)
        xc = _post(xc, flat(att_c), flat(yf_c), flat(yb_c), flat(u_c), z_c, hyo_cf, modtok, row_c, lw, TM)
```

```python
import functools
import math

import jax
import jax.numpy as jnp
import numpy as np
from jax import lax
from jax.experimental import pallas as pl
from jax.experimental.pallas import tpu as pltpu

F32 = jnp.float32
BF16 = jnp.bfloat16
HIGHEST = lax.Precision.HIGHEST

D_MODEL = 1024
DEPTH = 2
GRID_W = 64
EPS = 1e-6
N_HEADS = 6
D_NOPE = 64
D_ROPE = 32
D_QK = D_NOPE + D_ROPE
D_V = 64
Q_LORA = 256
KV_LORA = 128
ROPE_THETA = 10000.0
ROPE_FREQS = D_ROPE // 4
S_HEADS = 6
S_HDIM = 64
S_INNER = S_HEADS * S_HDIM
S_GROUPS = 2
S_STATE = 64
S_XBC = S_INNER + 2 * S_GROUPS * S_STATE
HY_CH = D_MODEL - N_HEADS * D_V - S_INNER
HY_BANDS = 16
HY_EMB = 1 + 2 * HY_BANDS
HY_HIDDEN = 64
HY_DECAY_PCT_SHORT = 0.3
HY_DECAY_PCT_LONG = 1.5
HY_DECAY_TARGET = 1e-2
D_FF = 4 * D_MODEL

LANES = 128
SUBLANES = 8
VMEM_LIMIT = 56 * 1024 * 1024

TM = 512
HALO = 16
N_INPROJ_IN = 22
SSD_GROUP = 4
TQ = 512
TK = 512
ATT_UNROLL = 8
CHUNK = 128
HEAD_PAD = LANES
HY_BLOCK = 1024
HY_FT = 512
HY_SUB = 256


def _cparams(*sem):
    return pltpu.CompilerParams(dimension_semantics=sem, vmem_limit_bytes=VMEM_LIMIT)


def _dot(a, b, precision=None):
    return jnp.dot(a, b, preferred_element_type=F32, precision=precision)


def _dot_nt(a, b):
    return lax.dot_general(a, b, (((1,), (1,)), ((), ())), preferred_element_type=F32)


def _dot_tn(a, b):
    return lax.dot_general(a, b, (((0,), (0,)), ((), ())), preferred_element_type=F32)


def _rms(x):
    return x * lax.rsqrt(jnp.mean(x * x, axis=-1, keepdims=True) + EPS)


def _sigmoid(x):
    return 1.0 / (1.0 + jnp.exp(-x))


def _full(arr):
    return pl.BlockSpec(arr.shape, lambda *_: (0,) * arr.ndim, pipeline_mode=pl.Buffered(1))


def _mod_kernel(c_ref, w_ref, b_ref, o_ref):
    cv = c_ref[...]
    s = (cv * _sigmoid(cv)).astype(BF16)
    o_ref[0] = _dot(s, w_ref[0].astype(BF16)) + b_ref[0]


def _modulation(cvec, w_mod, b_mod):
    tn = 1024
    ncol = w_mod.shape[-1]
    return pl.pallas_call(
        _mod_kernel,
        out_shape=jax.ShapeDtypeStruct((DEPTH, SUBLANES, ncol), F32),
        grid=(DEPTH, ncol // tn),
        in_specs=[pl.BlockSpec((SUBLANES, D_MODEL), lambda l, j: (0, 0)),
                  pl.BlockSpec((1, D_MODEL, tn), lambda l, j: (l, 0, j)),
                  pl.BlockSpec((1, 1, tn), lambda l, j: (l, 0, j))],
        out_specs=pl.BlockSpec((1, SUBLANES, tn), lambda l, j: (l, 0, j)),
        compiler_params=_cparams("parallel", "parallel"),
        name="modulation",
    )(cvec, w_mod, b_mod.reshape(DEPTH, 1, ncol))


def _modspec(row0, chunk):
    return pl.BlockSpec((1, 1, D_MODEL), lambda b, i: (row0 + b, 0, chunk))


def _inproj_kernel(*refs, use_rope, seq_len):
    (x_ref, xp_ref, xn_ref, sh_ref, sc_ref, g_ref, wa_ref, wz_ref, wx_ref, wdt_ref, why_ref, cx_ref, ch_ref,
     dtb_ref, gcq_ref, gckv_ref, wuq_ref, wk_ref, wv_ref, vone_ref, gq_ref, gk_ref) = refs[:N_INPROJ_IN]
    q_ref, k_ref, v_ref, z_ref, u_ref, dt_ref, hyt_ref = refs[-7:]
    tm = x_ref.shape[1]

    def normmod(xv):
        return ((_rms(xv) * g_ref[...]) * (1.0 + sc_ref[0]) + sh_ref[0]).astype(BF16)

    hb = normmod(x_ref[0])
    hb_ext = jnp.concatenate([normmod(xp_ref[0]), hb, normmod(xn_ref[0])], axis=0)
    pos = (pl.program_id(1) * tm + lax.broadcasted_iota(jnp.int32, (tm, 1), 0)) % seq_len
    has_prev = pos != 0
    has_next = pos != seq_len - 1

    def conv3(w_ref, taps_ref):
        ext = _dot(hb_ext, w_ref[...])
        n = ext.shape[0]
        up = pltpu.roll(ext, 1, 0)[HALO:HALO + tm]
        dn = pltpu.roll(ext, n - 1, 0)[HALO:HALO + tm]
        taps = taps_ref[...]
        return (jnp.where(has_prev, up, 0.0) * taps[0:1] + ext[HALO:HALO + tm] * taps[1:2]
                + jnp.where(has_next, dn, 0.0) * taps[2:3] + taps[3:4])

    z_ref[0] = _dot(hb, wz_ref[...])
    xc = conv3(wx_ref, cx_ref)
    u_ref[0] = xc * _sigmoid(xc)
    t = _dot(hb, wdt_ref[...]) + dtb_ref[...]
    dt_ref[0] = jnp.maximum(t, 0.0) + jnp.log(1.0 + jnp.exp(-jnp.abs(t)))
    hy_t = conv3(why_ref, ch_ref).T
    for j in range(3):
        hyt_ref[0, j] = hy_t[j * HY_CH:(j + 1) * HY_CH]
    a = _dot(hb, wa_ref[...])
    cq = a[:, :Q_LORA]
    ckv = a[:, Q_LORA:Q_LORA + KV_LORA]
    krb = a[:, Q_LORA + KV_LORA:Q_LORA + KV_LORA + HEAD_PAD]
    cqn = (_rms(cq) * gcq_ref[...]).astype(BF16)
    ckvn = (_rms(ckv) * gckv_ref[...]).astype(BF16)
    qr = _dot(cqn, wuq_ref[...])
    kn = _dot(ckvn, wk_ref[...])
    v_ref[0] = (_dot(ckvn, wv_ref[...]) + vone_ref[...]).astype(BF16)
    gq = gq_ref[...]
    gk = gk_ref[...]
    if use_rope:
        wuqp_ref, gqp_ref, gkp_ref, ct_ref, sn_ref = refs[N_INPROJ_IN:N_INPROJ_IN + 5]
        qp = _dot(cqn, wuqp_ref[...])
        krp = a[:, Q_LORA + KV_LORA + HEAD_PAD:]
        gqp = gqp_ref[...]
        gkp = gkp_ref[...]
        ct = ct_ref[...]
        sn = sn_ref[...]
        gq, gqp, gk, gkp = gq * ct, gqp * sn, gk * ct, gkp * sn

    def head_norm_rope(t, g, tp, gp):
        ss = jnp.sum(t * t, axis=-1, keepdims=True) * (1.0 / D_QK)
        inv = lax.rsqrt(ss + EPS)
        if not use_rope:
            return t * inv * g
        return (t * g + tp * gp) * inv

    qscale = math.log2(math.e) / math.sqrt(D_QK)
    for hh in range(N_HEADS):
        sl = slice(HEAD_PAD * hh, HEAD_PAD * (hh + 1))
        q_ref[0, :, sl] = (head_norm_rope(qr[:, sl], gq, qp[:, sl] if use_rope else None,
                                          gqp if use_rope else None) * qscale).astype(BF16)
        k_ref[0, :, sl] = head_norm_rope(kn[:, sl] + krb, gk, krp if use_rope else None,
                                         gkp if use_rope else None).astype(BF16)


def _inproj(x, modtok, mod_row0, lw, rope_tabs, tm, seq_len):
    bsz, ntok, _ = x.shape
    hw = N_HEADS * HEAD_PAD
    r = tm // HALO
    nhalo = ntok // HALO

    def tok(width, dtype):
        return jax.ShapeDtypeStruct((bsz, ntok, width), dtype)

    def tokspec(width):
        return pl.BlockSpec((1, tm, width), lambda b, i: (b, i, 0))

    weights = [lw["g_mix"], lw["wa"], lw["wz"], lw["wx"], lw["wdt"], lw["why"], lw["taps_x"], lw["taps_hy"],
               lw["dt_b"], lw["g_cq"], lw["g_ckv"], lw["wuq"], lw["wk"], lw["wv"], lw["vone"], lw["g_q"], lw["g_k"]]
    assert 5 + len(weights) == N_INPROJ_IN
    tabs = [] if rope_tabs is None else list(rope_tabs)
    if rope_tabs is not None:
        weights += [lw["wuqp"], lw["g_qp"], lw["g_kp"]]
    tabspec = pl.BlockSpec((tm, HEAD_PAD), lambda b, i: (i, 0))
    return pl.pallas_call(
        functools.partial(_inproj_kernel, use_rope=rope_tabs is not None, seq_len=seq_len),
        out_shape=[tok(hw, BF16), tok(hw, BF16), tok(hw, BF16), tok(S_INNER, F32), tok(S_XBC, F32),
                   tok(LANES, F32), jax.ShapeDtypeStruct((bsz, 3, HY_CH, ntok), F32)],
        grid=(bsz, ntok // tm),
        in_specs=[tokspec(D_MODEL),
                  pl.BlockSpec((1, HALO, D_MODEL), lambda b, i: (b, jnp.maximum(i * r - 1, 0), 0)),
                  pl.BlockSpec((1, HALO, D_MODEL), lambda b, i: (b, jnp.minimum((i + 1) * r, nhalo - 1), 0)),
                  _modspec(mod_row0, 0), _modspec(mod_row0, 1)]
        + [_full(w) for w in weights] + [tabspec] * len(tabs),
        out_specs=[tokspec(hw), tokspec(hw), tokspec(hw), tokspec(S_INNER), tokspec(S_XBC), tokspec(LANES),
                   pl.BlockSpec((1, 3, HY_CH, tm), lambda b, i: (b, 0, 0, i))],
        compiler_params=_cparams("parallel", "parallel"),
        name="inproj",
    )(x, x, x, modtok, modtok, *weights, *tabs)


def _attn_kernel(*refs, seg_rows):
    q_ref = refs[0]
    o_ref = refs[-1]
    tq = q_ref.shape[1]
    slices = [slice(HEAD_PAD * hh, HEAD_PAD * (hh + 1)) for hh in range(2)]

    def scores(k_ref, start, size):
        return tuple(_dot_nt(q_ref[0, :, sl], k_ref[0, pl.ds(start, size), sl]) for sl in slices)

    def consume(state, s, v_ref, start, size):
        new = []
        for hh in range(2):
            m, acc = state[hh]
            m_new = jnp.maximum(m, jnp.max(s[hh], axis=-1, keepdims=True))
            p = jnp.exp2(s[hh] - m_new)
            acc = jnp.exp2(m - m_new) * acc + _dot(p.astype(BF16), v_ref[0, pl.ds(start, size), slices[hh]])
            new.append((m_new, acc))
        return tuple(new)

    state = tuple((jnp.full((tq, 1), -jnp.inf, F32), jnp.zeros((tq, HEAD_PAD), F32)) for _ in range(2))
    for seg, rows in enumerate(seg_rows):
        k_ref = refs[1 + 2 * seg]
        v_ref = refs[2 + 2 * seg]
        n_full = rows // TK
        if n_full:
            def body(t, st, k_ref=k_ref, v_ref=v_ref):
                start = pl.multiple_of(t * TK, TK)
                return consume(st, scores(k_ref, start, TK), v_ref, start, TK)

            state = lax.fori_loop(0, n_full, body, state, unroll=ATT_UNROLL)
        if rows % TK:
            state = consume(state, scores(k_ref, n_full * TK, rows % TK), v_ref, n_full * TK, rows % TK)
    acc_e = state[0][1]
    acc_o = state[1][1]
    lane = lax.broadcasted_iota(jnp.int32, (tq, HEAD_PAD), 1)
    o_ref[0] = jnp.where(lane < D_V, acc_e / acc_e[:, D_V:D_V + 1], acc_o / acc_o[:, 0:1]).astype(BF16)


def _attention(q, kvs, tq):
    bsz, nq, _ = q.shape
    pw = 2 * HEAD_PAD
    in_specs = [pl.BlockSpec((1, tq, pw), lambda b, p, i: (b, i, p))]
    args = [q]
    for k, v in kvs:
        spec = pl.BlockSpec((1, k.shape[1], pw), lambda b, p, i: (b, 0, p))
        in_specs += [spec, spec]
        args += [k, v]
    return pl.pallas_call(
        functools.partial(_attn_kernel, seg_rows=tuple(k.shape[1] for k, _ in kvs)),
        out_shape=jax.ShapeDtypeStruct((bsz, nq, N_HEADS * D_V), BF16),
        grid=(bsz, N_HEADS // 2, nq // tq),
        in_specs=in_specs,
        out_specs=pl.BlockSpec((1, tq, 2 * D_V), lambda b, p, i: (b, i, p)),
        compiler_params=_cparams("parallel", "parallel", "parallel"),
        name="attention",
    )(*args)


def _split3(x):
    hi = x.astype(BF16)
    r1 = x - hi.astype(F32)
    mid = r1.astype(BF16)
    return hi, mid, (r1 - mid.astype(F32)).astype(BF16)


def _ssd_chunk(u, dt, a_row, st, reverse):
    off = S_HEADS if reverse else 0
    hpg = S_HEADS // S_GROUPS
    xs = u[:, :S_INNER]
    bmat = u[:, S_INNER:S_INNER + LANES]
    cmat = u[:, S_INNER + LANES:]
    ii = lax.broadcasted_iota(jnp.int32, (CHUNK, CHUNK), 0)
    jj = lax.broadcasted_iota(jnp.int32, (CHUNK, CHUNK), 1)
    tri = (jj >= ii) if reverse else (jj <= ii)
    tri16 = tri.astype(BF16)
    cum = sum(_dot(tri16, part) for part in _split3(dt * a_row))
    cum_t = cum.T
    dt_t = dt.T
    tot = cum[0:1] if reverse else cum[CHUNK - 1:CHUNK]
    lane = lax.broadcasted_iota(jnp.int32, (CHUNK, LANES), 1)
    lane2 = lax.broadcasted_iota(jnp.int32, (2 * CHUNK, LANES), 1)
    b16 = bmat.astype(BF16)
    bmat_t = bmat.T
    cms = [jnp.where((lane // S_STATE) == g, cmat, 0.0) for g in range(S_GROUPS)]
    cbs = [_dot_nt(cms[g].astype(BF16), b16) for g in range(S_GROUPS)]
    bts = [jnp.where((ii // S_STATE) == g, bmat_t, 0.0) for g in range(S_GROUPS)]
    ys = []
    sts = []
    for pair in range(S_HEADS // 2):
        sl = slice(LANES * pair, LANES * (pair + 1))
        both = jnp.concatenate([xs[:, sl], st[:, sl]], axis=0)
        acc = None
        new = None
        decay = []
        for half in range(2):
            hh = 2 * pair + half
            g = hh // hpg
            c = off + hh
            col = jnp.broadcast_to(cum[:, c:c + 1], (CHUNK, LANES))
            row = cum_t[c:c + 1, :]
            dtr = dt_t[c:c + 1, :]
            dec = jnp.where(tri, jnp.exp(jnp.where(tri, col - row, 0.0)), 0.0)
            lhs = jnp.concatenate([cbs[g] * dec * dtr, cms[g] * jnp.exp(col)], axis=1).astype(BF16)
            rhs = jnp.where((lane2 // S_HDIM) == half, both, 0.0).astype(BF16)
            part = _dot(lhs, rhs)
            acc = part if acc is None else acc + part
            tot_h = tot[:, c:c + 1]
            wrow = jnp.exp(tot_h - row) * dtr
            pn = _dot((bts[g] * wrow).astype(BF16), rhs[:CHUNK])
            new = pn if new is None else new + pn
            decay.append(jnp.exp(tot_h))
        ys.append(acc)
        sts.append(st[:, sl] * jnp.where(lane[0:1] < S_HDIM, decay[0], decay[1]) + new)
    return jnp.concatenate(ys, axis=1), jnp.concatenate(sts, axis=1)


def _ssd_kernel(uf_ref, dtf_ref, ub_ref, dtb_ref, a_ref, initf_ref, initb_ref,
                yf_ref, yb_ref, finf_ref, finb_ref, stf_ref, stb_ref, *, nchunks):
    @pl.when(pl.program_id(1) == 0)
    def _():
        stf_ref[...] = initf_ref[0]
        stb_ref[...] = initb_ref[0]

    a_row = a_ref[...]
    st = stf_ref[...]
    for k in range(nchunks):
        rows = slice(k * CHUNK, (k + 1) * CHUNK)
        y, st = _ssd_chunk(uf_ref[0, rows, :], dtf_ref[0, rows, :], a_row, st, False)
        yf_ref[0, rows, :] = y
    stf_ref[...] = st
    stf_last = st
    st = stb_ref[...]
    for k in reversed(range(nchunks)):
        rows = slice(k * CHUNK, (k + 1) * CHUNK)
        y, st = _ssd_chunk(ub_ref[0, rows, :], dtb_ref[0, rows, :], a_row, st, True)
        yb_ref[0, rows, :] = y
    stb_ref[...] = st

    @pl.when(pl.program_id(1) == pl.num_programs(1) - 1)
    def _():
        finf_ref[0] = stf_last
        finb_ref[0] = st


def _ssd(u, dt, a_row, init_f, init_b, nchunks):
    bsz, ntok, width = u.shape
    blk = nchunks * CHUNK
    nsteps = ntok // blk
    srows = S_GROUPS * S_STATE
    fwd = lambda b, s: (b, s, 0)
    bwd = lambda b, s: (b, nsteps - 1 - s, 0)
    state = pl.BlockSpec((1, srows, S_INNER), lambda b, s: (b, 0, 0))
    yshape = jax.ShapeDtypeStruct((bsz, ntok, S_INNER), F32)
    sshape = jax.ShapeDtypeStruct((bsz, srows, S_INNER), F32)
    return pl.pallas_call(
        functools.partial(_ssd_kernel, nchunks=nchunks),
        out_shape=[yshape, yshape, sshape, sshape],
        grid=(bsz, nsteps),
        in_specs=[pl.BlockSpec((1, blk, width), fwd), pl.BlockSpec((1, blk, LANES), fwd),
                  pl.BlockSpec((1, blk, width), bwd), pl.BlockSpec((1, blk, LANES), bwd),
                  _full(a_row), state, state],
        out_specs=[pl.BlockSpec((1, blk, S_INNER), fwd), pl.BlockSpec((1, blk, S_INNER), bwd), state, state],
        scratch_shapes=[pltpu.VMEM((srows, S_INNER), F32), pltpu.VMEM((srows, S_INNER), F32)],
        compiler_params=_cparams("parallel", "arbitrary"),
        name="ssd",
    )(u, dt, u, dt, a_row, init_f, init_b)


def _hyfilt_kernel(ft_ref, w1_ref, b1_ref, q1_ref, w2_ref, b2_ref, q2_ref, w3b_ref, w3f_ref, dl_ref, o_ref, h_ref):
    n = ft_ref.shape[1]
    half = n // 2

    @pl.when(pl.program_id(0) == 0)
    def _():
        h = jnp.sin(q1_ref[...] * (_dot(w1_ref[...], ft_ref[...], HIGHEST) + b1_ref[...]))
        h_ref[...] = jnp.sin(q2_ref[...] * (_dot(w2_ref[...], h, HIGHEST) + b2_ref[...]))

    fb = _dot(w3b_ref[...], h_ref[:, :half], HIGHEST)
    ff = _dot(w3f_ref[...], h_ref[:, half:], HIGHEST)
    f = jnp.concatenate([fb, ff], axis=1)
    f = f * jnp.exp(-ft_ref[0:1, :] * dl_ref[...])
    pos = lax.broadcasted_iota(jnp.int32, f.shape, 1)
    f = jnp.where(pos == 0, 0.0, f)
    o_ref[...] = f / (jnp.sum(jnp.abs(f), axis=1, keepdims=True) + EPS)


def _hyfilt(feats_t, hw):
    n = feats_t.shape[1]
    rows = 2 * HY_CH
    rt = 128
    args = [feats_t, hw["w1t"], hw["b1"], hw["q1"], hw["w2t"], hw["b2"], hw["q2"]]
    return pl.pallas_call(
        _hyfilt_kernel,
        out_shape=jax.ShapeDtypeStruct((rows, n), F32),
        grid=(rows // rt,),
        in_specs=[_full(a) for a in args]
        + [pl.BlockSpec((rt, HY_HIDDEN), lambda r: (r, 0)), pl.BlockSpec((rt, HY_HIDDEN), lambda r: (r, 0)),
           pl.BlockSpec((rt, 1), lambda r: (r, 0))],
        out_specs=pl.BlockSpec((rt, n), lambda r: (r, 0)),
        scratch_shapes=[pltpu.VMEM((HY_HIDDEN, n), F32)],
        compiler_params=_cparams("arbitrary"),
        name="hyena_filter",
    )(*args, hw["w3b"], hw["w3f"], hw["delta"])


def _hyspec_kernel(f_ref, c_ref, s_ref, g_ref, *, blk, nb):
    kt = pl.program_id(0)
    ftile = c_ref.shape[1]
    ctab = c_ref[...].astype(BF16)
    stab = s_ref[...].astype(BF16)
    freq = lax.broadcasted_iota(jnp.int32, (2 * HY_CH, ftile), 1) + kt * ftile
    sigma = jnp.where((freq & 1) == 0, 1.0, -1.0)
    scale = 2.0 / (2 * blk)
    prev = None
    for e in range(2 * nb):
        phi = f_ref[:, e * blk:(e + 1) * blk]
        p16 = phi.astype(BF16)
        a = _dot(p16, ctab)
        bs = _dot(p16, stab)
        cur = (a, bs, phi[:, 0:1])
        if prev is not None:
            gr = (a + sigma * prev[1]) * scale
            gi = (sigma * (prev[0] - prev[2]) - bs) * scale
            for o in range(2):
                g_ref[o, e - 1, 0] = gr[o * HY_CH:(o + 1) * HY_CH]
                g_ref[o, e - 1, 1] = gi[o * HY_CH:(o + 1) * HY_CH]
        prev = cur


def _hyspec(filt, ctab, stab, blk):
    n = filt.shape[1]
    nb = n // (2 * blk)
    ft = min(HY_SUB, blk)
    return pl.pallas_call(
        functools.partial(_hyspec_kernel, blk=blk, nb=nb),
        out_shape=jax.ShapeDtypeStruct((2, 2 * nb - 1, 2, HY_CH, blk), F32),
        grid=(blk // ft,),
        in_specs=[_full(filt),
                  pl.BlockSpec((blk, ft), lambda k: (0, k)),
                  pl.BlockSpec((blk, ft), lambda k: (0, k))],
        out_specs=pl.BlockSpec((2, 2 * nb - 1, 2, HY_CH, ft), lambda k: (0, 0, 0, 0, k)),
        compiler_params=_cparams("parallel"),
        name="hyena_spectra",
    )(filt, ctab, stab)


def _hyconv_kernel(u_ref, m_ref, d_ref, g_ref, c_ref, s_ref, o_ref, ub_ref, acc_ref, *, blk, nb):
    kt = pl.program_id(1)

    @pl.when(kt == 0)
    def _():
        for j in range(nb):
            ub_ref[j * HY_CH:(j + 1) * HY_CH, :] = u_ref[0, 0, :, j * blk:(j + 1) * blk].astype(BF16)
        acc_ref[...] = jnp.zeros_like(acc_ref)

    ub = ub_ref[...]
    total = None
    for f0 in range(0, c_ref.shape[1], HY_SUB):
        fs = slice(f0, f0 + HY_SUB)
        ctab = c_ref[:, fs]
        stab = s_ref[:, fs]
        xr = _dot(ub, ctab)
        xs = _dot(ub, stab)
        yrs = []
        yss = []
        for i in range(nb):
            yr = None
            ys = None
            for j in range(nb):
                d = i - j + nb - 1
                gr = g_ref[0, d, 0, :, fs]
                gi = g_ref[0, d, 1, :, fs]
                xrj = xr[j * HY_CH:(j + 1) * HY_CH]
                xsj = xs[j * HY_CH:(j + 1) * HY_CH]
                tr = gr * xrj + gi * xsj
                ts = gr * xsj - gi * xrj
                yr = tr if yr is None else yr + tr
                ys = ts if ys is None else ys + ts
            yrs.append(yr.astype(BF16))
            yss.append(ys.astype(BF16))
        part = _dot_nt(jnp.concatenate(yrs, axis=0), ctab) + _dot_nt(jnp.concatenate(yss, axis=0), stab)
        total = part if total is None else total + part
    acc_ref[...] += total

    @pl.when(kt == pl.num_programs(1) - 1)
    def _():
        for i in range(nb):
            cols = slice(i * blk, (i + 1) * blk)
            conv = acc_ref[i * HY_CH:(i + 1) * HY_CH, :]
            o_ref[0, :, cols] = (m_ref[0, 0, :, cols] * (conv + u_ref[0, 0, :, cols] * d_ref[...])).astype(o_ref.dtype)


def _hyconv(u4, usel, m4, msel, dcol, gspec, order, tabs, blk, out_dtype):
    bsz, _, _, n = u4.shape
    nb = n // blk
    ft = min(HY_FT, blk)
    ctab, stab = tabs
    return pl.pallas_call(
        functools.partial(_hyconv_kernel, blk=blk, nb=nb),
        out_shape=jax.ShapeDtypeStruct((bsz, HY_CH, n), out_dtype),
        grid=(bsz, blk // ft),
        in_specs=[pl.BlockSpec((1, 1, HY_CH, n), lambda b, k: (b, usel, 0, 0), pipeline_mode=pl.Buffered(1)),
                  pl.BlockSpec((1, 1, HY_CH, n), lambda b, k: (b, msel, 0, 0), pipeline_mode=pl.Buffered(1)),
                  pl.BlockSpec((HY_CH, 1), lambda b, k: (0, 0)),
                  pl.BlockSpec((1, 2 * nb - 1, 2, HY_CH, ft), lambda b, k: (order, 0, 0, 0, k)),
                  pl.BlockSpec((blk, ft), lambda b, k: (0, k)),
                  pl.BlockSpec((blk, ft), lambda b, k: (0, k))],
        out_specs=pl.BlockSpec((1, HY_CH, n), lambda b, k: (b, 0, 0)),
        scratch_shapes=[pltpu.VMEM((nb * HY_CH, blk), BF16), pltpu.VMEM((nb * HY_CH, blk), F32)],
        compiler_params=_cparams("parallel", "arbitrary"),
        name="hyena_conv",
    )(u4, m4, dcol, gspec, ctab, stab)


def _dft_tables(blk):
    s = np.arange(blk, dtype=np.int64)[:, None]
    k = np.arange(blk, dtype=np.int64)[None, :]
    ang = ((s * (2 * k + 1)) % (4 * blk)).astype(np.float64) * (2.0 * math.pi / (4 * blk))
    return jnp.asarray(np.cos(ang), F32), jnp.asarray(np.sin(ang), F32)


def _hyena_features(length):
    p = np.arange(length)
    tb = np.where(p == 0, 0, length - p)
    t = np.concatenate([tb, p]).astype(np.float64)
    t01 = t / (length - 1)
    w = (2.0 * math.pi / length) * t
    bands = np.linspace(1e-4, HY_BANDS - 1, HY_BANDS)[:, None]
    feats = np.concatenate([t01[None, :], np.cos(bands * w[None, :]), -np.sin(bands * w[None, :])], axis=0)
    pad = (-feats.shape[0]) % SUBLANES
    return jnp.asarray(np.pad(feats, ((0, pad), (0, 0))), F32)


def _hyena(p4, hw, tabs, blk, out_dtype):
    bsz, _, _, n = p4.shape
    filt = _hyfilt(_hyena_features(n), hw)
    gspec = _hyspec(filt, tabs[0], tabs[1], blk)
    z1 = _hyconv(p4, 0, p4, 1, hw["d0"], gspec, 0, tabs, blk, F32)
    return _hyconv(z1.reshape(bsz, 1, HY_CH, n), 0, p4, 2, hw["d1"], gspec, 1, tabs, blk, out_dtype)


def _post_kernel(x_ref, att_ref, yf_ref, yb_ref, xs_ref, z_ref, hyo_ref, ga1_ref, sh2_ref, sc2_ref, ga2_ref,
                 dsk_ref, gss_ref, wo_ref, g2_ref, w1_ref, w2_ref, o_ref):
    x = x_ref[0]
    y = yf_ref[0] + yb_ref[0] + xs_ref[0] * dsk_ref[...]
    zz = z_ref[0]
    y = y * (zz * _sigmoid(zz))
    gw = S_INNER // S_GROUPS
    lane = lax.broadcasted_iota(jnp.int32, y.shape, 1)
    first = lane < gw
    y2 = y * y
    s0 = jnp.sum(jnp.where(first, y2, 0.0), axis=-1, keepdims=True)
    s1 = jnp.sum(y2, axis=-1, keepdims=True) - s0
    inv = jnp.where(first, lax.rsqrt(s0 * (1.0 / gw) + EPS), lax.rsqrt(s1 * (1.0 / gw) + EPS))
    ssm = (y * inv * gss_ref[...]).astype(BF16)
    na = N_HEADS * D_V
    mix = _dot(att_ref[0], wo_ref[0:na]) + _dot(ssm, wo_ref[na:na + S_INNER])
    mix = mix + _dot_tn(hyo_ref[0], wo_ref[na + S_INNER:])
    x1 = x + ga1_ref[0] * mix
    h2 = (_rms(x1) * g2_ref[...]) * (1.0 + sc2_ref[0]) + sh2_ref[0]
    hb = h2.astype(BF16)
    acc = jnp.zeros_like(x1)
    fc = 1024
    for c in range(D_FF // fc):
        t = jnp.maximum(_dot(hb, w1_ref[:, c * fc:(c + 1) * fc]), 0.0)
        acc = acc + _dot((t * t).astype(BF16), w2_ref[c * fc:(c + 1) * fc, :])
    o_ref[0] = x1 + ga2_ref[0] * acc


def _post(x, att, yf, yb, u, z, hyo, modtok, mod_row0, lw, tm):
    bsz, ntok, _ = x.shape

    def tokspec(width):
        return pl.BlockSpec((1, tm, width), lambda b, i: (b, i, 0))

    weights = [lw["dskip"], lw["g_ssm"], lw["wo"], lw["g_mlp"], lw["w1"], lw["w2"]]
    return pl.pallas_call(
        _post_kernel,
        out_shape=jax.ShapeDtypeStruct((bsz, ntok, D_MODEL), F32),
        grid=(bsz, ntok // tm),
        in_specs=[tokspec(D_MODEL), tokspec(N_HEADS * D_V), tokspec(S_INNER), tokspec(S_INNER), tokspec(S_INNER),
                  tokspec(S_INNER), pl.BlockSpec((1, HY_CH, tm), lambda b, i: (b, 0, i)),
                  _modspec(mod_row0, 2), _modspec(mod_row0, 3), _modspec(mod_row0, 4), _modspec(mod_row0, 5)]
        + [_full(w) for w in weights],
        out_specs=tokspec(D_MODEL),
        compiler_params=_cparams("parallel", "parallel"),
        name="post",
    )(x, att, yf, yb, u, z, hyo, modtok, modtok, modtok, modtok, *weights)


def _layer_weights(i, p):
    o = np.cumsum([0, Q_LORA, KV_LORA, D_ROPE, S_INNER, S_XBC, 2 * S_HEADS, 3 * HY_CH])
    w_in = p["w_in"][i]
    wcq, wckv, wkr, wz, wx, wdt, why = (w_in[:, o[j]:o[j + 1]] for j in range(7))
    ropepad = ((0, 0), (D_NOPE, HEAD_PAD - D_QK))
    partner = np.arange(D_ROPE) ^ ROPE_FREQS
    wkr_pad = jnp.concatenate([jnp.pad(wkr, ropepad), jnp.pad(wkr[:, partner], ropepad)], axis=1)
    wuq3 = p["w_uq"][i].reshape(Q_LORA, N_HEADS, D_QK)
    wuq = jnp.pad(wuq3, ((0, 0), (0, 0), (0, HEAD_PAD - D_QK)))
    wuqp = jnp.pad(wuq3[:, :, D_NOPE:][:, :, partner], ((0, 0), (0, 0), (D_NOPE, HEAD_PAD - D_QK)))
    wukv = p["w_ukv"][i].reshape(KV_LORA, N_HEADS, D_NOPE + D_V)
    wk = jnp.pad(wukv[:, :, :D_NOPE], ((0, 0), (0, 0), (0, HEAD_PAD - D_NOPE)))
    wv_lo = jnp.pad(wukv[:, :, D_NOPE:], ((0, 0), (0, 0), (0, HEAD_PAD - D_V)))
    wv_hi = jnp.pad(wukv[:, :, D_NOPE:], ((0, 0), (0, 0), (HEAD_PAD - D_V, 0)))
    odd = (np.arange(N_HEADS) % 2 == 1)[None, :, None]
    wv = jnp.where(odd, wv_hi, wv_lo)
    hw = N_HEADS * HEAD_PAD
    vone = np.zeros((N_HEADS, HEAD_PAD), np.float32)
    vone[0::2, D_V] = 1.0
    vone[1::2, 0] = 1.0

    def headgain(g):
        return jnp.pad(g, (0, HEAD_PAD - D_QK)).reshape(1, HEAD_PAD)

    def partnergain(g):
        return jnp.pad(g[D_NOPE:][partner], (D_NOPE, HEAD_PAD - D_QK)).reshape(1, HEAD_PAD)

    lw = dict(
        g_mix=p["g_norm_mix"][i].reshape(1, D_MODEL),
        wa=jnp.concatenate([wcq, wckv, wkr_pad], axis=1).astype(BF16),
        wz=wz.astype(BF16), wx=wx.astype(BF16),
        wdt=jnp.pad(wdt, ((0, 0), (0, LANES - 2 * S_HEADS))).astype(BF16),
        why=why.astype(BF16),
        g_cq=p["g_cq"][i].reshape(1, Q_LORA), g_ckv=p["g_ckv"][i].reshape(1, KV_LORA),
        wuq=wuq.reshape(Q_LORA, hw).astype(BF16), wk=wk.reshape(KV_LORA, hw).astype(BF16),
        wv=wv.reshape(KV_LORA, hw).astype(BF16), vone=jnp.asarray(vone.reshape(1, hw)),
        g_q=headgain(p["g_qhead"][i]), g_k=headgain(p["g_khead"][i]),
        wuqp=wuqp.reshape(Q_LORA, hw).astype(BF16),
        g_qp=partnergain(p["g_qhead"][i]), g_kp=partnergain(p["g_khead"][i]),
        dskip=jnp.repeat(p["d_skip_ssm"][i], S_HDIM).reshape(1, S_INNER),
        g_ssm=p["g_ssm_out"][i].reshape(1, S_INNER),
        wo=p["w_out"][i].astype(BF16), g_mlp=p["g_norm_mlp"][i].reshape(1, D_MODEL),
        w1=p["w_ff1"][i].astype(BF16), w2=p["w_ff2"][i].astype(BF16),
    )
    a = -jnp.exp(p["a_log"][i].astype(F32)).reshape(1, -1)
    lw["a_row"] = jnp.pad(a, ((0, 0), (0, LANES - 2 * S_HEADS)))
    lw["taps_x"] = jnp.pad(jnp.concatenate([p["w_conv_ssm"][i], p["b_conv_ssm"][i][None, :]], axis=0),
                           ((0, SUBLANES - 4), (0, 0)))
    lw["taps_hy"] = jnp.pad(jnp.concatenate([p["w_conv_hy"][i], p["b_conv_hy"][i][None, :]], axis=0),
                            ((0, SUBLANES - 4), (0, 0)))
    lw["dt_b"] = jnp.pad(p["dt_bias"][i].reshape(1, -1), ((0, 0), (0, LANES - 2 * S_HEADS)))
    lw["hy"] = dict(
        w1t=jnp.pad(p["w_f1"][i].T, ((0, 0), (0, (-HY_EMB) % SUBLANES))),
        b1=p["b_f1"][i].reshape(HY_HIDDEN, 1), q1=p["freq_f1"][i].reshape(HY_HIDDEN, 1),
        w2t=p["w_f2"][i].T, b2=p["b_f2"][i].reshape(HY_HIDDEN, 1), q2=p["freq_f2"][i].reshape(HY_HIDDEN, 1),
        w3f=p["w_f3"][i][:, :2 * HY_CH].T, w3b=p["w_f3"][i][:, 2 * HY_CH:].T,
        delta=jnp.asarray(np.tile(np.abs(np.linspace(math.log(HY_DECAY_TARGET) / HY_DECAY_PCT_LONG,
                                                     math.log(HY_DECAY_TARGET) / HY_DECAY_PCT_SHORT, HY_CH)),
                                  2).reshape(2 * HY_CH, 1), F32),
        d0=p["d_skip_hy"][i][0].reshape(HY_CH, 1), d1=p["d_skip_hy"][i][1].reshape(HY_CH, 1),
    )
    return lw


def _rope_tables(seq):
    pos = np.arange(seq)
    inv = ROPE_THETA ** (-np.arange(ROPE_FREQS, dtype=np.float64) / ROPE_FREQS)
    ang = np.stack([(pos // GRID_W)[:, None] * inv, (pos % GRID_W)[:, None] * inv], axis=1)
    cos, sin = np.cos(ang), np.sin(ang)
    ct = np.ones((seq, HEAD_PAD))
    sn = np.zeros((seq, HEAD_PAD))
    for axis in range(2):
        lo = D_NOPE + axis * 2 * ROPE_FREQS
        mid = lo + ROPE_FREQS
        ct[:, lo:mid] = cos[:, axis]
        ct[:, mid:mid + ROPE_FREQS] = cos[:, axis]
        sn[:, lo:mid] = -sin[:, axis]
        sn[:, mid:mid + ROPE_FREQS] = sin[:, axis]
    return jnp.asarray(ct, F32), jnp.asarray(sn, F32)


def kernel(x, c, ctx, c_ctx, w_mod, b_mod, g_norm_mix, g_norm_mlp, w_in, w_out, g_cq, g_ckv, w_uq, w_ukv, g_qhead, g_khead, w_conv_ssm, b_conv_ssm, a_log, dt_bias, d_skip_ssm, g_ssm_out, w_conv_hy, b_conv_hy, w_f1, b_f1, freq_f1, w_f2, b_f2, freq_f2, w_f3, d_skip_hy, w_ff1, w_ff2):
    params = dict(w_in=w_in, w_out=w_out, g_norm_mix=g_norm_mix, g_norm_mlp=g_norm_mlp, g_cq=g_cq, g_ckv=g_ckv,
                  w_uq=w_uq, w_ukv=w_ukv, g_qhead=g_qhead, g_khead=g_khead, a_log=a_log, d_skip_ssm=d_skip_ssm,
                  g_ssm_out=g_ssm_out, w_conv_hy=w_conv_hy, b_conv_hy=b_conv_hy, w_f1=w_f1, b_f1=b_f1,
                  freq_f1=freq_f1, w_f2=w_f2, b_f2=b_f2, freq_f2=freq_f2, w_f3=w_f3, d_skip_hy=d_skip_hy,
                  w_ff1=w_ff1, w_ff2=w_ff2, w_conv_ssm=w_conv_ssm, b_conv_ssm=b_conv_ssm, dt_bias=dt_bias)
    bsz, seq, _ = x.shape
    nctx = ctx.shape[1]
    assert seq % HY_BLOCK == 0 and seq % TM == 0 and (bsz * nctx) % TM == 0 and TM % nctx == 0
    assert bsz + 1 <= SUBLANES and CHUNK == S_GROUPS * S_STATE and nctx % CHUNK == 0

    cvec = jnp.zeros((SUBLANES, D_MODEL), F32).at[:bsz].set(c).at[bsz].set(c_ctx)
    mod = _modulation(cvec, w_mod, b_mod)
    modtok = mod.reshape(DEPTH * SUBLANES, 1, 6 * D_MODEL)

    rope_tabs = _rope_tables(seq)
    tabs_lat = tuple(t.astype(BF16) for t in _dft_tables(HY_BLOCK))
    tabs_ctx = tuple(t.astype(BF16) for t in _dft_tables(nctx))
    zero_state = jnp.zeros((bsz, S_GROUPS * S_STATE, S_INNER), F32)

    xl = x
    xc = ctx.reshape(1, bsz * nctx, D_MODEL)
    for i in range(DEPTH):
        last = i == DEPTH - 1
        lw = _layer_weights(i, params)
        row_l = i * SUBLANES
        row_c = i * SUBLANES + bsz

        q_c, k_c, v_c, z_c, u_c, dt_c, hyt_c = _inproj(xc, modtok, row_c, lw, None, TM, nctx)
        per_b = lambda t: t.reshape(bsz, nctx, t.shape[-1])
        k_c, v_c, u_c, dt_c = per_b(k_c), per_b(v_c), per_b(u_c), per_b(dt_c)
        yf_c, yb_c, s_fwd, s_bwd = _ssd(u_c, dt_c, lw["a_row"], zero_state, zero_state, nctx // CHUNK)

        q, k, v, z, u, dt, hyt = _inproj(xl, modtok, row_l, lw, rope_tabs, TM, seq)
        att = _attention(q, [(k, v), (k_c, v_c)], TQ)
        yf, yb, _, _ = _ssd(u, dt, lw["a_row"], s_fwd, s_bwd, SSD_GROUP)
        hyo = _hyena(hyt, lw["hy"], tabs_lat, HY_BLOCK, BF16)
        xl = _post(xl, att, yf, yb, u, z, hyo, modtok, row_l, lw, TM)
        if last:
            return xl

        att_c = _attention(per_b(q_c), [(k_c, v_c)], nctx)
        hyt_cb = hyt_c.reshape(3, HY_CH, bsz, nctx).transpose(2, 0, 1, 3)
        hyo_c = _hyena(hyt_cb, lw["hy"], tabs_ctx, nctx, BF16)
        flat = lambda t: t.reshape(1, bsz * nctx, t.shape[-1])
        hyo_cf = hyo_c.transpose(1, 0, 2).reshape(1, HY_CH, bsz * nctx)
        xc = _post(xc, flat(att_c), flat(yf_c), flat(yb_c), flat(u_c), z_c, hyo_cf, modtok, row_c, lw, TM)
```

```python
import functools
import math

import jax
import jax.numpy as jnp
import numpy as np
from jax import lax
from jax.experimental import pallas as pl
from jax.experimental.pallas import tpu as pltpu

F32 = jnp.float32
BF16 = jnp.bfloat16
HIGHEST = lax.Precision.HIGHEST

D_MODEL = 1024
DEPTH = 2
GRID_W = 64
EPS = 1e-6
N_HEADS = 6
D_NOPE = 64
D_ROPE = 32
D_QK = D_NOPE + D_ROPE
D_V = 64
Q_LORA = 256
KV_LORA = 128
ROPE_THETA = 10000.0
ROPE_FREQS = D_ROPE // 4
S_HEADS = 6
S_HDIM = 64
S_INNER = S_HEADS * S_HDIM
S_GROUPS = 2
S_STATE = 64
S_XBC = S_INNER + 2 * S_GROUPS * S_STATE
HY_CH = D_MODEL - N_HEADS * D_V - S_INNER
HY_BANDS = 16
HY_EMB = 1 + 2 * HY_BANDS
HY_HIDDEN = 64
HY_DECAY_PCT_SHORT = 0.3
HY_DECAY_PCT_LONG = 1.5
HY_DECAY_TARGET = 1e-2
D_FF = 4 * D_MODEL

LANES = 128
SUBLANES = 8
VMEM_LIMIT = 56 * 1024 * 1024

TM = 512
HALO = 16
N_INPROJ_IN = 22
SSD_GROUP = 4
TQ = 1024
TK = 512
ATT_UNROLL = 8
CHUNK = 128
HEAD_PAD = LANES
HY_BLOCK = 1024
HY_FT = 256
HY_SUB = 256


def _cparams(*sem):
    return pltpu.CompilerParams(dimension_semantics=sem, vmem_limit_bytes=VMEM_LIMIT)


def _dot(a, b, precision=None):
    return jnp.dot(a, b, preferred_element_type=F32, precision=precision)


def _dot_nt(a, b):
    return lax.dot_general(a, b, (((1,), (1,)), ((), ())), preferred_element_type=F32)


def _dot_tn(a, b):
    return lax.dot_general(a, b, (((0,), (0,)), ((), ())), preferred_element_type=F32)


def _rms(x):
    return x * lax.rsqrt(jnp.mean(x * x, axis=-1, keepdims=True) + EPS)


def _sigmoid(x):
    return 1.0 / (1.0 + jnp.exp(-x))


def _full(arr):
    return pl.BlockSpec(arr.shape, lambda *_: (0,) * arr.ndim, pipeline_mode=pl.Buffered(1))


def _mod_kernel(c_ref, w_ref, b_ref, o_ref):
    cv = c_ref[...]
    s = (cv * _sigmoid(cv)).astype(BF16)
    o_ref[0] = _dot(s, w_ref[0].astype(BF16)) + b_ref[0]


def _modulation(cvec, w_mod, b_mod):
    tn = 1024
    ncol = w_mod.shape[-1]
    return pl.pallas_call(
        _mod_kernel,
        out_shape=jax.ShapeDtypeStruct((DEPTH, SUBLANES, ncol), F32),
        grid=(DEPTH, ncol // tn),
        in_specs=[pl.BlockSpec((SUBLANES, D_MODEL), lambda l, j: (0, 0)),
                  pl.BlockSpec((1, D_MODEL, tn), lambda l, j: (l, 0, j)),
                  pl.BlockSpec((1, 1, tn), lambda l, j: (l, 0, j))],
        out_specs=pl.BlockSpec((1, SUBLANES, tn), lambda l, j: (l, 0, j)),
        compiler_params=_cparams("parallel", "parallel"),
        name="modulation",
    )(cvec, w_mod, b_mod.reshape(DEPTH, 1, ncol))


def _modspec(row0, chunk):
    return pl.BlockSpec((1, 1, D_MODEL), lambda b, i: (row0 + b, 0, chunk))


def _inproj_kernel(*refs, use_rope, seq_len):
    (x_ref, xp_ref, xn_ref, sh_ref, sc_ref, g_ref, wa_ref, wz_ref, wx_ref, wdt_ref, why_ref, cx_ref, ch_ref,
     dtb_ref, gcq_ref, gckv_ref, wuq_ref, wk_ref, wv_ref, vone_ref, gq_ref, gk_ref) = refs[:N_INPROJ_IN]
    q_ref, k_ref, v_ref, z_ref, u_ref, dt_ref, hyt_ref = refs[-7:]
    tm = x_ref.shape[1]

    def normmod(xv):
        return ((_rms(xv) * g_ref[...]) * (1.0 + sc_ref[0]) + sh_ref[0]).astype(BF16)

    hb = normmod(x_ref[0])
    hb_ext = jnp.concatenate([normmod(xp_ref[0]), hb, normmod(xn_ref[0])], axis=0)
    pos = (pl.program_id(1) * tm + lax.broadcasted_iota(jnp.int32, (tm, 1), 0)) % seq_len
    has_prev = pos != 0
    has_next = pos != seq_len - 1

    def conv3(w_ref, taps_ref):
        ext = _dot(hb_ext, w_ref[...])
        n = ext.shape[0]
        up = pltpu.roll(ext, 1, 0)[HALO:HALO + tm]
        dn = pltpu.roll(ext, n - 1, 0)[HALO:HALO + tm]
        taps = taps_ref[...]
        return (jnp.where(has_prev, up, 0.0) * taps[0:1] + ext[HALO:HALO + tm] * taps[1:2]
                + jnp.where(has_next, dn, 0.0) * taps[2:3] + taps[3:4])

    z_ref[0] = _dot(hb, wz_ref[...])
    xc = conv3(wx_ref, cx_ref)
    u_ref[0] = xc * _sigmoid(xc)
    t = _dot(hb, wdt_ref[...]) + dtb_ref[...]
    dt_ref[0] = jnp.maximum(t, 0.0) + jnp.log(1.0 + jnp.exp(-jnp.abs(t)))
    hy_t = conv3(why_ref, ch_ref).T
    for j in range(3):
        hyt_ref[0, j] = hy_t[j * HY_CH:(j + 1) * HY_CH]
    a = _dot(hb, wa_ref[...])
    cq = a[:, :Q_LORA]
    ckv = a[:, Q_LORA:Q_LORA + KV_LORA]
    krb = a[:, Q_LORA + KV_LORA:Q_LORA + KV_LORA + HEAD_PAD]
    cqn = (_rms(cq) * gcq_ref[...]).astype(BF16)
    ckvn = (_rms(ckv) * gckv_ref[...]).astype(BF16)
    qr = _dot(cqn, wuq_ref[...])
    kn = _dot(ckvn, wk_ref[...])
    v_ref[0] = (_dot(ckvn, wv_ref[...]) + vone_ref[...]).astype(BF16)
    gq = gq_ref[...]
    gk = gk_ref[...]
    if use_rope:
        wuqp_ref, gqp_ref, gkp_ref, ct_ref, sn_ref = refs[N_INPROJ_IN:N_INPROJ_IN + 5]
        qp = _dot(cqn, wuqp_ref[...])
        krp = a[:, Q_LORA + KV_LORA + HEAD_PAD:]
        gqp = gqp_ref[...]
        gkp = gkp_ref[...]
        ct = ct_ref[...]
        sn = sn_ref[...]
        gq, gqp, gk, gkp = gq * ct, gqp * sn, gk * ct, gkp * sn

    def head_norm_rope(t, g, tp, gp):
        ss = jnp.sum(t * t, axis=-1, keepdims=True) * (1.0 / D_QK)
        inv = lax.rsqrt(ss + EPS)
        if not use_rope:
            return t * inv * g
        return (t * g + tp * gp) * inv

    qscale = math.log2(math.e) / math.sqrt(D_QK)
    for hh in range(N_HEADS):
        sl = slice(HEAD_PAD * hh, HEAD_PAD * (hh + 1))
        q_ref[0, :, sl] = (head_norm_rope(qr[:, sl], gq, qp[:, sl] if use_rope else None,
                                          gqp if use_rope else None) * qscale).astype(BF16)
        k_ref[0, :, sl] = head_norm_rope(kn[:, sl] + krb, gk, krp if use_rope else None,
                                         gkp if use_rope else None).astype(BF16)


def _inproj(x, modtok, mod_row0, lw, rope_tabs, tm, seq_len):
    bsz, ntok, _ = x.shape
    hw = N_HEADS * HEAD_PAD
    r = tm // HALO
    nhalo = ntok // HALO

    def tok(width, dtype):
        return jax.ShapeDtypeStruct((bsz, ntok, width), dtype)

    def tokspec(width):
        return pl.BlockSpec((1, tm, width), lambda b, i: (b, i, 0))

    weights = [lw["g_mix"], lw["wa"], lw["wz"], lw["wx"], lw["wdt"], lw["why"], lw["taps_x"], lw["taps_hy"],
               lw["dt_b"], lw["g_cq"], lw["g_ckv"], lw["wuq"], lw["wk"], lw["wv"], lw["vone"], lw["g_q"], lw["g_k"]]
    assert 5 + len(weights) == N_INPROJ_IN
    tabs = [] if rope_tabs is None else list(rope_tabs)
    if rope_tabs is not None:
        weights += [lw["wuqp"], lw["g_qp"], lw["g_kp"]]
    tabspec = pl.BlockSpec((tm, HEAD_PAD), lambda b, i: (i, 0))
    return pl.pallas_call(
        functools.partial(_inproj_kernel, use_rope=rope_tabs is not None, seq_len=seq_len),
        out_shape=[tok(hw, BF16), tok(hw, BF16), tok(hw, BF16), tok(S_INNER, F32), tok(S_XBC, F32),
                   tok(LANES, F32), jax.ShapeDtypeStruct((bsz, 3, HY_CH, ntok), F32)],
        grid=(bsz, ntok // tm),
        in_specs=[tokspec(D_MODEL),
                  pl.BlockSpec((1, HALO, D_MODEL), lambda b, i: (b, jnp.maximum(i * r - 1, 0), 0)),
                  pl.BlockSpec((1, HALO, D_MODEL), lambda b, i: (b, jnp.minimum((i + 1) * r, nhalo - 1), 0)),
                  _modspec(mod_row0, 0), _modspec(mod_row0, 1)]
        + [_full(w) for w in weights] + [tabspec] * len(tabs),
        out_specs=[tokspec(hw), tokspec(hw), tokspec(hw), tokspec(S_INNER), tokspec(S_XBC), tokspec(LANES),
                   pl.BlockSpec((1, 3, HY_CH, tm), lambda b, i: (b, 0, 0, i))],
        compiler_params=_cparams("parallel", "parallel"),
        name="inproj",
    )(x, x, x, modtok, modtok, *weights, *tabs)


def _attn_kernel(*refs, seg_rows):
    q_ref = refs[0]
    o_ref = refs[-1]
    tq = q_ref.shape[1]
    slices = [slice(HEAD_PAD * hh, HEAD_PAD * (hh + 1)) for hh in range(2)]

    def scores(k_ref, start, size):
        return tuple(_dot_nt(q_ref[0, :, sl], k_ref[0, pl.ds(start, size), sl]) for sl in slices)

    def consume(state, s, v_ref, start, size):
        new = []
        for hh in range(2):
            m, acc = state[hh]
            m_new = jnp.maximum(m, jnp.max(s[hh], axis=-1, keepdims=True))
            p = jnp.exp2(s[hh] - m_new)
            acc = jnp.exp2(m - m_new) * acc + _dot(p.astype(BF16), v_ref[0, pl.ds(start, size), slices[hh]])
            new.append((m_new, acc))
        return tuple(new)

    state = tuple((jnp.full((tq, 1), -jnp.inf, F32), jnp.zeros((tq, HEAD_PAD), F32)) for _ in range(2))
    for seg, rows in enumerate(seg_rows):
        k_ref = refs[1 + 2 * seg]
        v_ref = refs[2 + 2 * seg]
        n_full = rows // TK
        if n_full:
            def body(t, st, k_ref=k_ref, v_ref=v_ref):
                start = pl.multiple_of(t * TK, TK)
                return consume(st, scores(k_ref, start, TK), v_ref, start, TK)

            state = lax.fori_loop(0, n_full, body, state, unroll=ATT_UNROLL)
        if rows % TK:
            state = consume(state, scores(k_ref, n_full * TK, rows % TK), v_ref, n_full * TK, rows % TK)
    acc_e = state[0][1]
    acc_o = state[1][1]
    lane = lax.broadcasted_iota(jnp.int32, (tq, HEAD_PAD), 1)
    o_ref[0] = jnp.where(lane < D_V, acc_e / acc_e[:, D_V:D_V + 1], acc_o / acc_o[:, 0:1]).astype(BF16)


def _attention(q, kvs, tq):
    bsz, nq, _ = q.shape
    pw = 2 * HEAD_PAD
    in_specs = [pl.BlockSpec((1, tq, pw), lambda b, p, i: (b, i, p))]
    args = [q]
    for k, v in kvs:
        spec = pl.BlockSpec((1, k.shape[1], pw), lambda b, p, i: (b, 0, p))
        in_specs += [spec, spec]
        args += [k, v]
    return pl.pallas_call(
        functools.partial(_attn_kernel, seg_rows=tuple(k.shape[1] for k, _ in kvs)),
        out_shape=jax.ShapeDtypeStruct((bsz, nq, N_HEADS * D_V), BF16),
        grid=(bsz, N_HEADS // 2, nq // tq),
        in_specs=in_specs,
        out_specs=pl.BlockSpec((1, tq, 2 * D_V), lambda b, p, i: (b, i, p)),
        compiler_params=_cparams("parallel", "parallel", "parallel"),
        name="attention",
    )(*args)


def _split3(x):
    hi = x.astype(BF16)
    r1 = x - hi.astype(F32)
    mid = r1.astype(BF16)
    return hi, mid, (r1 - mid.astype(F32)).astype(BF16)


def _ssd_chunk(u, dt, a_row, st, reverse):
    off = S_HEADS if reverse else 0
    hpg = S_HEADS // S_GROUPS
    xs = u[:, :S_INNER]
    bmat = u[:, S_INNER:S_INNER + LANES]
    cmat = u[:, S_INNER + LANES:]
    ii = lax.broadcasted_iota(jnp.int32, (CHUNK, CHUNK), 0)
    jj = lax.broadcasted_iota(jnp.int32, (CHUNK, CHUNK), 1)
    tri = (jj >= ii) if reverse else (jj <= ii)
    tri16 = tri.astype(BF16)
    cum = sum(_dot(tri16, part) for part in _split3(dt * a_row))
    cum_t = cum.T
    dt_t = dt.T
    tot = cum[0:1] if reverse else cum[CHUNK - 1:CHUNK]
    lane = lax.broadcasted_iota(jnp.int32, (CHUNK, LANES), 1)
    lane2 = lax.broadcasted_iota(jnp.int32, (2 * CHUNK, LANES), 1)
    b16 = bmat.astype(BF16)
    bmat_t = bmat.T
    cms = [jnp.where((lane // S_STATE) == g, cmat, 0.0) for g in range(S_GROUPS)]
    cbs = [jnp.where(tri, _dot_nt(cms[g].astype(BF16), b16), 0.0) for g in range(S_GROUPS)]
    bts = [jnp.where((ii // S_STATE) == g, bmat_t, 0.0) for g in range(S_GROUPS)]
    ys = []
    sts = []
    for pair in range(S_HEADS // 2):
        sl = slice(LANES * pair, LANES * (pair + 1))
        both = jnp.concatenate([xs[:, sl], st[:, sl]], axis=0)
        acc = None
        new = None
        decay = []
        for half in range(2):
            hh = 2 * pair + half
            g = hh // hpg
            c = off + hh
            col = jnp.broadcast_to(cum[:, c:c + 1], (CHUNK, LANES))
            row = cum_t[c:c + 1, :]
            dtr = dt_t[c:c + 1, :]
            dec = jnp.exp(jnp.minimum(col - row, 0.0))
            lhs = jnp.concatenate([cbs[g] * dec * dtr, cms[g] * jnp.exp(col)], axis=1).astype(BF16)
            rhs = jnp.where((lane2 // S_HDIM) == half, both, 0.0).astype(BF16)
            part = _dot(lhs, rhs)
            acc = part if acc is None else acc + part
            tot_h = tot[:, c:c + 1]
            wrow = jnp.exp(tot_h - row) * dtr
            pn = _dot((bts[g] * wrow).astype(BF16), rhs[:CHUNK])
            new = pn if new is None else new + pn
            decay.append(jnp.exp(tot_h))
        ys.append(acc)
        sts.append(st[:, sl] * jnp.where(lane[0:1] < S_HDIM, decay[0], decay[1]) + new)
    return jnp.concatenate(ys, axis=1), jnp.concatenate(sts, axis=1)


def _ssd_kernel(uf_ref, dtf_ref, ub_ref, dtb_ref, a_ref, initf_ref, initb_ref,
                yf_ref, yb_ref, finf_ref, finb_ref, stf_ref, stb_ref, *, nchunks):
    @pl.when(pl.program_id(1) == 0)
    def _():
        stf_ref[...] = initf_ref[0]
        stb_ref[...] = initb_ref[0]

    a_row = a_ref[...]
    st = stf_ref[...]
    for k in range(nchunks):
        rows = slice(k * CHUNK, (k + 1) * CHUNK)
        y, st = _ssd_chunk(uf_ref[0, rows, :], dtf_ref[0, rows, :], a_row, st, False)
        yf_ref[0, rows, :] = y
    stf_ref[...] = st
    stf_last = st
    st = stb_ref[...]
    for k in reversed(range(nchunks)):
        rows = slice(k * CHUNK, (k + 1) * CHUNK)
        y, st = _ssd_chunk(ub_ref[0, rows, :], dtb_ref[0, rows, :], a_row, st, True)
        yb_ref[0, rows, :] = y
    stb_ref[...] = st

    @pl.when(pl.program_id(1) == pl.num_programs(1) - 1)
    def _():
        finf_ref[0] = stf_last
        finb_ref[0] = st


def _ssd(u, dt, a_row, init_f, init_b, nchunks):
    bsz, ntok, width = u.shape
    blk = nchunks * CHUNK
    nsteps = ntok // blk
    srows = S_GROUPS * S_STATE
    fwd = lambda b, s: (b, s, 0)
    bwd = lambda b, s: (b, nsteps - 1 - s, 0)
    state = pl.BlockSpec((1, srows, S_INNER), lambda b, s: (b, 0, 0))
    yshape = jax.ShapeDtypeStruct((bsz, ntok, S_INNER), F32)
    sshape = jax.ShapeDtypeStruct((bsz, srows, S_INNER), F32)
    return pl.pallas_call(
        functools.partial(_ssd_kernel, nchunks=nchunks),
        out_shape=[yshape, yshape, sshape, sshape],
        grid=(bsz, nsteps),
        in_specs=[pl.BlockSpec((1, blk, width), fwd), pl.BlockSpec((1, blk, LANES), fwd),
                  pl.BlockSpec((1, blk, width), bwd), pl.BlockSpec((1, blk, LANES), bwd),
                  _full(a_row), state, state],
        out_specs=[pl.BlockSpec((1, blk, S_INNER), fwd), pl.BlockSpec((1, blk, S_INNER), bwd), state, state],
        scratch_shapes=[pltpu.VMEM((srows, S_INNER), F32), pltpu.VMEM((srows, S_INNER), F32)],
        compiler_params=_cparams("parallel", "arbitrary"),
        name="ssd",
    )(u, dt, u, dt, a_row, init_f, init_b)


def _hyfilt_kernel(ft_ref, w1_ref, b1_ref, q1_ref, w2_ref, b2_ref, q2_ref, w3b_ref, w3f_ref, dl_ref, o_ref, h_ref):
    n = ft_ref.shape[1]
    half = n // 2

    @pl.when(pl.program_id(0) == 0)
    def _():
        h = jnp.sin(q1_ref[...] * (_dot(w1_ref[...], ft_ref[...], HIGHEST) + b1_ref[...]))
        h_ref[...] = jnp.sin(q2_ref[...] * (_dot(w2_ref[...], h, HIGHEST) + b2_ref[...]))

    fb = _dot(w3b_ref[...], h_ref[:, :half], HIGHEST)
    ff = _dot(w3f_ref[...], h_ref[:, half:], HIGHEST)
    f = jnp.concatenate([fb, ff], axis=1)
    f = f * jnp.exp(-ft_ref[0:1, :] * dl_ref[...])
    pos = lax.broadcasted_iota(jnp.int32, f.shape, 1)
    f = jnp.where(pos == 0, 0.0, f)
    o_ref[...] = f / (jnp.sum(jnp.abs(f), axis=1, keepdims=True) + EPS)


def _hyfilt(feats_t, hw):
    n = feats_t.shape[1]
    rows = 2 * HY_CH
    rt = 128
    args = [feats_t, hw["w1t"], hw["b1"], hw["q1"], hw["w2t"], hw["b2"], hw["q2"]]
    return pl.pallas_call(
        _hyfilt_kernel,
        out_shape=jax.ShapeDtypeStruct((rows, n), F32),
        grid=(rows // rt,),
        in_specs=[_full(a) for a in args]
        + [pl.BlockSpec((rt, HY_HIDDEN), lambda r: (r, 0)), pl.BlockSpec((rt, HY_HIDDEN), lambda r: (r, 0)),
           pl.BlockSpec((rt, 1), lambda r: (r, 0))],
        out_specs=pl.BlockSpec((rt, n), lambda r: (r, 0)),
        scratch_shapes=[pltpu.VMEM((HY_HIDDEN, n), F32)],
        compiler_params=_cparams("arbitrary"),
        name="hyena_filter",
    )(*args, hw["w3b"], hw["w3f"], hw["delta"])


def _hyspec_kernel(f_ref, c_ref, s_ref, g_ref, *, blk, nb):
    kt = pl.program_id(0)
    ftile = c_ref.shape[1]
    ctab = c_ref[...].astype(BF16)
    stab = s_ref[...].astype(BF16)
    freq = lax.broadcasted_iota(jnp.int32, (2 * HY_CH, ftile), 1) + kt * ftile
    sigma = jnp.where((freq & 1) == 0, 1.0, -1.0)
    scale = 2.0 / (2 * blk)
    prev = None
    for e in range(2 * nb):
        phi = f_ref[:, e * blk:(e + 1) * blk]
        p16 = phi.astype(BF16)
        a = _dot(p16, ctab)
        bs = _dot(p16, stab)
        cur = (a, bs, phi[:, 0:1])
        if prev is not None:
            gr = (a + sigma * prev[1]) * scale
            gi = (sigma * (prev[0] - prev[2]) - bs) * scale
            for o in range(2):
                g_ref[o, e - 1, 0] = gr[o * HY_CH:(o + 1) * HY_CH]
                g_ref[o, e - 1, 1] = gi[o * HY_CH:(o + 1) * HY_CH]
        prev = cur


def _hyspec(filt, ctab, stab, blk):
    n = filt.shape[1]
    nb = n // (2 * blk)
    ft = min(HY_SUB, blk)
    return pl.pallas_call(
        functools.partial(_hyspec_kernel, blk=blk, nb=nb),
        out_shape=jax.ShapeDtypeStruct((2, 2 * nb - 1, 2, HY_CH, blk), F32),
        grid=(blk // ft,),
        in_specs=[_full(filt),
                  pl.BlockSpec((blk, ft), lambda k: (0, k)),
                  pl.BlockSpec((blk, ft), lambda k: (0, k))],
        out_specs=pl.BlockSpec((2, 2 * nb - 1, 2, HY_CH, ft), lambda k: (0, 0, 0, 0, k)),
        compiler_params=_cparams("parallel"),
        name="hyena_spectra",
    )(filt, ctab, stab)


def _hyconv_kernel(u_ref, m_ref, d_ref, g_ref, c_ref, s_ref, o_ref, ub_ref, acc_ref, *, blk, nb):
    kt = pl.program_id(1)

    @pl.when(kt == 0)
    def _():
        for j in range(nb):
            ub_ref[j * HY_CH:(j + 1) * HY_CH, :] = u_ref[0, 0, :, j * blk:(j + 1) * blk].astype(BF16)
        acc_ref[...] = jnp.zeros_like(acc_ref)

    ub = ub_ref[...]
    total = None
    for f0 in range(0, c_ref.shape[1], HY_SUB):
        fs = slice(f0, f0 + HY_SUB)
        ctab = c_ref[:, fs]
        stab = s_ref[:, fs]
        xr = _dot(ub, ctab)
        xs = _dot(ub, stab)
        yrs = []
        yss = []
        for i in range(nb):
            yr = None
            ys = None
            for j in range(nb):
                d = i - j + nb - 1
                gr = g_ref[0, d, 0, :, fs]
                gi = g_ref[0, d, 1, :, fs]
                xrj = xr[j * HY_CH:(j + 1) * HY_CH]
                xsj = xs[j * HY_CH:(j + 1) * HY_CH]
                tr = gr * xrj + gi * xsj
                ts = gr * xsj - gi * xrj
                yr = tr if yr is None else yr + tr
                ys = ts if ys is None else ys + ts
            yrs.append(yr.astype(BF16))
            yss.append(ys.astype(BF16))
        part = _dot_nt(jnp.concatenate(yrs, axis=0), ctab) + _dot_nt(jnp.concatenate(yss, axis=0), stab)
        total = part if total is None else total + part
    acc_ref[...] += total

    @pl.when(kt == pl.num_programs(1) - 1)
    def _():
        for i in range(nb):
            cols = slice(i * blk, (i + 1) * blk)
            conv = acc_ref[i * HY_CH:(i + 1) * HY_CH, :]
            o_ref[0, :, cols] = (m_ref[0, 0, :, cols] * (conv + u_ref[0, 0, :, cols] * d_ref[...])).astype(o_ref.dtype)


def _hyconv(u4, usel, m4, msel, dcol, gspec, order, tabs, blk, out_dtype):
    bsz, _, _, n = u4.shape
    nb = n // blk
    ft = min(HY_FT, blk)
    ctab, stab = tabs
    return pl.pallas_call(
        functools.partial(_hyconv_kernel, blk=blk, nb=nb),
        out_shape=jax.ShapeDtypeStruct((bsz, HY_CH, n), out_dtype),
        grid=(bsz, blk // ft),
        in_specs=[pl.BlockSpec((1, 1, HY_CH, n), lambda b, k: (b, usel, 0, 0)),
                  pl.BlockSpec((1, 1, HY_CH, n), lambda b, k: (b, msel, 0, 0)),
                  pl.BlockSpec((HY_CH, 1), lambda b, k: (0, 0)),
                  pl.BlockSpec((1, 2 * nb - 1, 2, HY_CH, ft), lambda b, k: (order, 0, 0, 0, k)),
                  pl.BlockSpec((blk, ft), lambda b, k: (0, k)),
                  pl.BlockSpec((blk, ft), lambda b, k: (0, k))],
        out_specs=pl.BlockSpec((1, HY_CH, n), lambda b, k: (b, 0, 0)),
        scratch_shapes=[pltpu.VMEM((nb * HY_CH, blk), BF16), pltpu.VMEM((nb * HY_CH, blk), F32)],
        compiler_params=_cparams("parallel", "arbitrary"),
        name="hyena_conv",
    )(u4, m4, dcol, gspec, ctab, stab)


def _dft_tables(blk):
    s = np.arange(blk, dtype=np.int64)[:, None]
    k = np.arange(blk, dtype=np.int64)[None, :]
    ang = ((s * (2 * k + 1)) % (4 * blk)).astype(np.float64) * (2.0 * math.pi / (4 * blk))
    return jnp.asarray(np.cos(ang), F32), jnp.asarray(np.sin(ang), F32)


def _hyena_features(length):
    p = np.arange(length)
    tb = np.where(p == 0, 0, length - p)
    t = np.concatenate([tb, p]).astype(np.float64)
    t01 = t / (length - 1)
    w = (2.0 * math.pi / length) * t
    bands = np.linspace(1e-4, HY_BANDS - 1, HY_BANDS)[:, None]
    feats = np.concatenate([t01[None, :], np.cos(bands * w[None, :]), -np.sin(bands * w[None, :])], axis=0)
    pad = (-feats.shape[0]) % SUBLANES
    return jnp.asarray(np.pad(feats, ((0, pad), (0, 0))), F32)


def _hyena(p4, hw, tabs, blk, out_dtype):
    bsz, _, _, n = p4.shape
    filt = _hyfilt(_hyena_features(n), hw)
    gspec = _hyspec(filt, tabs[0], tabs[1], blk)
    z1 = _hyconv(p4, 0, p4, 1, hw["d0"], gspec, 0, tabs, blk, F32)
    return _hyconv(z1.reshape(bsz, 1, HY_CH, n), 0, p4, 2, hw["d1"], gspec, 1, tabs, blk, out_dtype)


def _post_kernel(x_ref, att_ref, yf_ref, yb_ref, xs_ref, z_ref, hyo_ref, ga1_ref, sh2_ref, sc2_ref, ga2_ref,
                 dsk_ref, gss_ref, wo_ref, g2_ref, w1_ref, w2_ref, o_ref):
    x = x_ref[0]
    y = yf_ref[0] + yb_ref[0] + xs_ref[0] * dsk_ref[...]
    zz = z_ref[0]
    y = y * (zz * _sigmoid(zz))
    gw = S_INNER // S_GROUPS
    lane = lax.broadcasted_iota(jnp.int32, y.shape, 1)
    first = lane < gw
    y2 = y * y
    s0 = jnp.sum(jnp.where(first, y2, 0.0), axis=-1, keepdims=True)
    s1 = jnp.sum(y2, axis=-1, keepdims=True) - s0
    inv = jnp.where(first, lax.rsqrt(s0 * (1.0 / gw) + EPS), lax.rsqrt(s1 * (1.0 / gw) + EPS))
    ssm = (y * inv * gss_ref[...]).astype(BF16)
    na = N_HEADS * D_V
    mix = _dot(att_ref[0], wo_ref[0:na]) + _dot(ssm, wo_ref[na:na + S_INNER])
    mix = mix + _dot_tn(hyo_ref[0], wo_ref[na + S_INNER:])
    x1 = x + ga1_ref[0] * mix
    h2 = (_rms(x1) * g2_ref[...]) * (1.0 + sc2_ref[0]) + sh2_ref[0]
    hb = h2.astype(BF16)
    acc = jnp.zeros_like(x1)
    fc = 1024
    for c in range(D_FF // fc):
        t = jnp.maximum(_dot(hb, w1_ref[:, c * fc:(c + 1) * fc]), 0.0)
        acc = acc + _dot((t * t).astype(BF16), w2_ref[c * fc:(c + 1) * fc, :])
    o_ref[0] = x1 + ga2_ref[0] * acc


def _post(x, att, yf, yb, u, z, hyo, modtok, mod_row0, lw, tm):
    bsz, ntok, _ = x.shape

    def tokspec(width):
        return pl.BlockSpec((1, tm, width), lambda b, i: (b, i, 0))

    weights = [lw["dskip"], lw["g_ssm"], lw["wo"], lw["g_mlp"], lw["w1"], lw["w2"]]
    return pl.pallas_call(
        _post_kernel,
        out_shape=jax.ShapeDtypeStruct((bsz, ntok, D_MODEL), F32),
        grid=(bsz, ntok // tm),
        in_specs=[tokspec(D_MODEL), tokspec(N_HEADS * D_V), tokspec(S_INNER), tokspec(S_INNER), tokspec(S_INNER),
                  tokspec(S_INNER), pl.BlockSpec((1, HY_CH, tm), lambda b, i: (b, 0, i)),
                  _modspec(mod_row0, 2), _modspec(mod_row0, 3), _modspec(mod_row0, 4), _modspec(mod_row0, 5)]
        + [_full(w) for w in weights],
        out_specs=tokspec(D_MODEL),
        compiler_params=_cparams("parallel", "parallel"),
        name="post",
    )(x, att, yf, yb, u, z, hyo, modtok, modtok, modtok, modtok, *weights)


def _layer_weights(i, p):
    o = np.cumsum([0, Q_LORA, KV_LORA, D_ROPE, S_INNER, S_XBC, 2 * S_HEADS, 3 * HY_CH])
    w_in = p["w_in"][i]
    wcq, wckv, wkr, wz, wx, wdt, why = (w_in[:, o[j]:o[j + 1]] for j in range(7))
    ropepad = ((0, 0), (D_NOPE, HEAD_PAD - D_QK))
    def partner(t):
        f = ROPE_FREQS
        return jnp.concatenate([t[..., (j ^ 1) * f:((j ^ 1) + 1) * f] for j in range(D_ROPE // f)], axis=-1)

    wkr_pad = jnp.concatenate([jnp.pad(wkr, ropepad), jnp.pad(partner(wkr), ropepad)], axis=1)
    wuq3 = p["w_uq"][i].reshape(Q_LORA, N_HEADS, D_QK)
    wuq = jnp.pad(wuq3, ((0, 0), (0, 0), (0, HEAD_PAD - D_QK)))
    wuqp = jnp.pad(partner(wuq3[:, :, D_NOPE:]), ((0, 0), (0, 0), (D_NOPE, HEAD_PAD - D_QK)))
    wukv = p["w_ukv"][i].reshape(KV_LORA, N_HEADS, D_NOPE + D_V)
    wk = jnp.pad(wukv[:, :, :D_NOPE], ((0, 0), (0, 0), (0, HEAD_PAD - D_NOPE)))
    wv_lo = jnp.pad(wukv[:, :, D_NOPE:], ((0, 0), (0, 0), (0, HEAD_PAD - D_V)))
    wv_hi = jnp.pad(wukv[:, :, D_NOPE:], ((0, 0), (0, 0), (HEAD_PAD - D_V, 0)))
    odd = (np.arange(N_HEADS) % 2 == 1)[None, :, None]
    wv = jnp.where(odd, wv_hi, wv_lo)
    hw = N_HEADS * HEAD_PAD
    vone = np.zeros((N_HEADS, HEAD_PAD), np.float32)
    vone[0::2, D_V] = 1.0
    vone[1::2, 0] = 1.0

    def headgain(g):
        return jnp.pad(g, (0, HEAD_PAD - D_QK)).reshape(1, HEAD_PAD)

    def partnergain(g):
        return jnp.pad(partner(g[D_NOPE:]), (D_NOPE, HEAD_PAD - D_QK)).reshape(1, HEAD_PAD)

    lw = dict(
        g_mix=p["g_norm_mix"][i].reshape(1, D_MODEL),
        wa=jnp.concatenate([wcq, wckv, wkr_pad], axis=1).astype(BF16),
        wz=wz.astype(BF16), wx=wx.astype(BF16),
        wdt=jnp.pad(wdt, ((0, 0), (0, LANES - 2 * S_HEADS))).astype(BF16),
        why=why.astype(BF16),
        g_cq=p["g_cq"][i].reshape(1, Q_LORA), g_ckv=p["g_ckv"][i].reshape(1, KV_LORA),
        wuq=wuq.reshape(Q_LORA, hw).astype(BF16), wk=wk.reshape(KV_LORA, hw).astype(BF16),
        wv=wv.reshape(KV_LORA, hw).astype(BF16), vone=jnp.asarray(vone.reshape(1, hw)),
        g_q=headgain(p["g_qhead"][i]), g_k=headgain(p["g_khead"][i]),
        wuqp=wuqp.reshape(Q_LORA, hw).astype(BF16),
        g_qp=partnergain(p["g_qhead"][i]), g_kp=partnergain(p["g_khead"][i]),
        dskip=jnp.repeat(p["d_skip_ssm"][i], S_HDIM).reshape(1, S_INNER),
        g_ssm=p["g_ssm_out"][i].reshape(1, S_INNER),
        wo=p["w_out"][i].astype(BF16), g_mlp=p["g_norm_mlp"][i].reshape(1, D_MODEL),
        w1=p["w_ff1"][i].astype(BF16), w2=p["w_ff2"][i].astype(BF16),
    )
    a = -jnp.exp(p["a_log"][i].astype(F32)).reshape(1, -1)
    lw["a_row"] = jnp.pad(a, ((0, 0), (0, LANES - 2 * S_HEADS)))
    lw["taps_x"] = jnp.pad(jnp.concatenate([p["w_conv_ssm"][i], p["b_conv_ssm"][i][None, :]], axis=0),
                           ((0, SUBLANES - 4), (0, 0)))
    lw["taps_hy"] = jnp.pad(jnp.concatenate([p["w_conv_hy"][i], p["b_conv_hy"][i][None, :]], axis=0),
                            ((0, SUBLANES - 4), (0, 0)))
    lw["dt_b"] = jnp.pad(p["dt_bias"][i].reshape(1, -1), ((0, 0), (0, LANES - 2 * S_HEADS)))
    lw["hy"] = dict(
        w1t=jnp.pad(p["w_f1"][i].T, ((0, 0), (0, (-HY_EMB) % SUBLANES))),
        b1=p["b_f1"][i].reshape(HY_HIDDEN, 1), q1=p["freq_f1"][i].reshape(HY_HIDDEN, 1),
        w2t=p["w_f2"][i].T, b2=p["b_f2"][i].reshape(HY_HIDDEN, 1), q2=p["freq_f2"][i].reshape(HY_HIDDEN, 1),
        w3f=p["w_f3"][i][:, :2 * HY_CH].T, w3b=p["w_f3"][i][:, 2 * HY_CH:].T,
        delta=jnp.asarray(np.tile(np.abs(np.linspace(math.log(HY_DECAY_TARGET) / HY_DECAY_PCT_LONG,
                                                     math.log(HY_DECAY_TARGET) / HY_DECAY_PCT_SHORT, HY_CH)),
                                  2).reshape(2 * HY_CH, 1), F32),
        d0=p["d_skip_hy"][i][0].reshape(HY_CH, 1), d1=p["d_skip_hy"][i][1].reshape(HY_CH, 1),
    )
    return lw


def _rope_tables(seq):
    pos = np.arange(seq)
    inv = ROPE_THETA ** (-np.arange(ROPE_FREQS, dtype=np.float64) / ROPE_FREQS)
    ang = np.stack([(pos // GRID_W)[:, None] * inv, (pos % GRID_W)[:, None] * inv], axis=1)
    cos, sin = np.cos(ang), np.sin(ang)
    ct = np.ones((seq, HEAD_PAD))
    sn = np.zeros((seq, HEAD_PAD))
    for axis in range(2):
        lo = D_NOPE + axis * 2 * ROPE_FREQS
        mid = lo + ROPE_FREQS
        ct[:, lo:mid] = cos[:, axis]
        ct[:, mid:mid + ROPE_FREQS] = cos[:, axis]
        sn[:, lo:mid] = -sin[:, axis]
        sn[:, mid:mid + ROPE_FREQS] = sin[:, axis]
    return jnp.asarray(ct, F32), jnp.asarray(sn, F32)


def kernel(x, c, ctx, c_ctx, w_mod, b_mod, g_norm_mix, g_norm_mlp, w_in, w_out, g_cq, g_ckv, w_uq, w_ukv, g_qhead, g_khead, w_conv_ssm, b_conv_ssm, a_log, dt_bias, d_skip_ssm, g_ssm_out, w_conv_hy, b_conv_hy, w_f1, b_f1, freq_f1, w_f2, b_f2, freq_f2, w_f3, d_skip_hy, w_ff1, w_ff2):
    params = dict(w_in=w_in, w_out=w_out, g_norm_mix=g_norm_mix, g_norm_mlp=g_norm_mlp, g_cq=g_cq, g_ckv=g_ckv,
                  w_uq=w_uq, w_ukv=w_ukv, g_qhead=g_qhead, g_khead=g_khead, a_log=a_log, d_skip_ssm=d_skip_ssm,
                  g_ssm_out=g_ssm_out, w_conv_hy=w_conv_hy, b_conv_hy=b_conv_hy, w_f1=w_f1, b_f1=b_f1,
                  freq_f1=freq_f1, w_f2=w_f2, b_f2=b_f2, freq_f2=freq_f2, w_f3=w_f3, d_skip_hy=d_skip_hy,
                  w_ff1=w_ff1, w_ff2=w_ff2, w_conv_ssm=w_conv_ssm, b_conv_ssm=b_conv_ssm, dt_bias=dt_bias)
    bsz, seq, _ = x.shape
    nctx = ctx.shape[1]
    assert seq % HY_BLOCK == 0 and seq % TM == 0 and (bsz * nctx) % TM == 0 and TM % nctx == 0
    assert bsz + 1 <= SUBLANES and CHUNK == S_GROUPS * S_STATE and nctx % CHUNK == 0

    cvec = jnp.zeros((SUBLANES, D_MODEL), F32).at[:bsz].set(c).at[bsz].set(c_ctx)
    mod = _modulation(cvec, w_mod, b_mod)
    modtok = mod.reshape(DEPTH * SUBLANES, 1, 6 * D_MODEL)

    rope_tabs = _rope_tables(seq)
    tabs_lat = tuple(t.astype(BF16) for t in _dft_tables(HY_BLOCK))
    tabs_ctx = tuple(t.astype(BF16) for t in _dft_tables(nctx))
    zero_state = jnp.zeros((bsz, S_GROUPS * S_STATE, S_INNER), F32)

    xl = x
    xc = ctx.reshape(1, bsz * nctx, D_MODEL)
    for i in range(DEPTH):
        last = i == DEPTH - 1
        lw = _layer_weights(i, params)
        row_l = i * SUBLANES
        row_c = i * SUBLANES + bsz

        q_c, k_c, v_c, z_c, u_c, dt_c, hyt_c = _inproj(xc, modtok, row_c, lw, None, TM, nctx)
        per_b = lambda t: t.reshape(bsz, nctx, t.shape[-1])
        k_c, v_c, u_c, dt_c = per_b(k_c), per_b(v_c), per_b(u_c), per_b(dt_c)
        yf_c, yb_c, s_fwd, s_bwd = _ssd(u_c, dt_c, lw["a_row"], zero_state, zero_state, nctx // CHUNK)

        q, k, v, z, u, dt, hyt = _inproj(xl, modtok, row_l, lw, rope_tabs, TM, seq)
        att = _attention(q, [(k, v), (k_c, v_c)], TQ)
        yf, yb, _, _ = _ssd(u, dt, lw["a_row"], s_fwd, s_bwd, SSD_GROUP)
        hyo = _hyena(hyt, lw["hy"], tabs_lat, HY_BLOCK, BF16)
        xl = _post(xl, att, yf, yb, u, z, hyo, modtok, row_l, lw, TM)
        if last:
            return xl

        att_c = _attention(per_b(q_c), [(k_c, v_c)], nctx)
        hyt_cb = hyt_c.reshape(3, HY_CH, bsz, nctx).transpose(2, 0, 1, 3)
        hyo_c = _hyena(hyt_cb, lw["hy"], tabs_ctx, nctx, BF16)
        flat = lambda t: t.reshape(1, bsz * nctx, t.shape[-1])
        hyo_cf = hyo_c.transpose(1, 0, 2).reshape(1, HY_CH, bsz * nctx)
        xc = _post(xc, flat(att_c), flat(yf_c), flat(yb_c), flat(u_c), z_c, hyo_cf, modtok, row_c, lw, TM)
```

```python
import functools
import math

import jax
import jax.numpy as jnp
import numpy as np
from jax import lax
from jax.experimental import pallas as pl
from jax.experimental.pallas import tpu as pltpu

F32 = jnp.float32
BF16 = jnp.bfloat16
HIGHEST = lax.Precision.HIGHEST

D_MODEL = 1024
DEPTH = 2
GRID_W = 64
EPS = 1e-6
N_HEADS = 6
D_NOPE = 64
D_ROPE = 32
D_QK = D_NOPE + D_ROPE
D_V = 64
Q_LORA = 256
KV_LORA = 128
ROPE_THETA = 10000.0
ROPE_FREQS = D_ROPE // 4
S_HEADS = 6
S_HDIM = 64
S_INNER = S_HEADS * S_HDIM
S_GROUPS = 2
S_STATE = 64
S_XBC = S_INNER + 2 * S_GROUPS * S_STATE
HY_CH = D_MODEL - N_HEADS * D_V - S_INNER
HY_BANDS = 16
HY_EMB = 1 + 2 * HY_BANDS
HY_HIDDEN = 64
HY_DECAY_PCT_SHORT = 0.3
HY_DECAY_PCT_LONG = 1.5
HY_DECAY_TARGET = 1e-2
D_FF = 4 * D_MODEL

LANES = 128
SUBLANES = 8
VMEM_LIMIT = 56 * 1024 * 1024

TM = 512
HALO = 16
N_INPROJ_IN = 18
W_A = Q_LORA + KV_LORA + 2 * LANES
W_ALL = W_A + S_INNER + S_XBC + 3 * HY_CH + LANES
SSD_GROUP = 4
SSD_BATCH = 2
TQ = 1024
TK = 512
ATT_UNROLL = 8
CHUNK = 128
HEAD_PAD = LANES
HY_BLOCK = 1024
HY_FT = 256
HY_SUB = 256


def _cparams(*sem):
    return pltpu.CompilerParams(dimension_semantics=sem, vmem_limit_bytes=VMEM_LIMIT)


def _dot(a, b, precision=None):
    return jnp.dot(a, b, preferred_element_type=F32, precision=precision)


def _dot_nt(a, b):
    return lax.dot_general(a, b, (((1,), (1,)), ((), ())), preferred_element_type=F32)


def _dot_tn(a, b):
    return lax.dot_general(a, b, (((0,), (0,)), ((), ())), preferred_element_type=F32)


def _rms(x):
    return x * lax.rsqrt(jnp.mean(x * x, axis=-1, keepdims=True) + EPS)


def _sigmoid(x):
    return 1.0 / (1.0 + jnp.exp(-x))


def _full(arr):
    return pl.BlockSpec(arr.shape, lambda *_: (0,) * arr.ndim, pipeline_mode=pl.Buffered(1))


def _layer(arr, layer):
    return pl.BlockSpec((None,) + arr.shape[1:], lambda *_: (layer,) + (0,) * (arr.ndim - 1),
                        pipeline_mode=pl.Buffered(1))


def _mod_kernel(c_ref, w_ref, b_ref, o_ref):
    cv = c_ref[...]
    s = (cv * _sigmoid(cv)).astype(BF16)
    o_ref[0] = _dot(s, w_ref[0].astype(BF16)) + b_ref[0]


def _modulation(cvec, w_mod, b_mod):
    tn = 1024
    ncol = w_mod.shape[-1]
    return pl.pallas_call(
        _mod_kernel,
        out_shape=jax.ShapeDtypeStruct((DEPTH, SUBLANES, ncol), F32),
        grid=(DEPTH, ncol // tn),
        in_specs=[pl.BlockSpec((SUBLANES, D_MODEL), lambda l, j: (0, 0)),
                  pl.BlockSpec((1, D_MODEL, tn), lambda l, j: (l, 0, j)),
                  pl.BlockSpec((1, 1, tn), lambda l, j: (l, 0, j))],
        out_specs=pl.BlockSpec((1, SUBLANES, tn), lambda l, j: (l, 0, j)),
        compiler_params=_cparams("parallel", "parallel"),
        name="modulation",
    )(cvec, w_mod, b_mod.reshape(DEPTH, 1, ncol))


def _modspec(row0, chunk):
    return pl.BlockSpec((1, 1, D_MODEL), lambda b, i: (row0 + b, 0, chunk))


def _inproj_kernel(*refs, use_rope, seq_len):
    (x_ref, xp_ref, xn_ref, sh_ref, sc_ref, g_ref, wall_ref, cx_ref, ch_ref,
     dtb_ref, gcq_ref, gckv_ref, wuq_ref, wk_ref, wv_ref, vone_ref, gq_ref, gk_ref) = refs[:N_INPROJ_IN]
    q_ref, k_ref, v_ref, z_ref, u_ref, dt_ref, hyt_ref = refs[-7:]
    tm = x_ref.shape[1]

    def normmod(xv):
        return ((_rms(xv) * g_ref[...]) * (1.0 + sc_ref[0]) + sh_ref[0]).astype(BF16)

    i = pl.program_id(1)
    hp = normmod(xp_ref[0])
    hn = normmod(xn_ref[0])
    edges_only = seq_len % tm == 0
    if edges_only:
        tiles_per_seq = seq_len // tm
        hp = jnp.where(i % tiles_per_seq == 0, jnp.zeros_like(hp), hp)
        hn = jnp.where(i % tiles_per_seq == tiles_per_seq - 1, jnp.zeros_like(hn), hn)
    else:
        pos = (i * tm + lax.broadcasted_iota(jnp.int32, (tm, 1), 0)) % seq_len
        has_prev = pos != 0
        has_next = pos != seq_len - 1
    hb_ext = jnp.concatenate([hp, normmod(x_ref[0]), hn], axis=0)
    proj = _dot_nt(hb_ext, wall_ref[...])
    main = slice(HALO, HALO + tm)

    def conv3(ext, taps_ref):
        n = ext.shape[0]
        up = pltpu.roll(ext, 1, 0)[main]
        dn = pltpu.roll(ext, n - 1, 0)[main]
        if not edges_only:
            up = jnp.where(has_prev, up, 0.0)
            dn = jnp.where(has_next, dn, 0.0)
        taps = taps_ref[...]
        return up * taps[0:1] + ext[main] * taps[1:2] + dn * taps[2:3] + taps[3:4]

    o_z = W_A
    o_x = o_z + S_INNER
    o_hy = o_x + S_XBC
    o_dt = o_hy + 3 * HY_CH
    z_ref[0] = proj[main, o_z:o_x]
    xc = conv3(proj[:, o_x:o_hy], cx_ref)
    u_ref[0] = xc * _sigmoid(xc)
    t = proj[main, o_dt:] + dtb_ref[...]
    dt_ref[0] = jnp.maximum(t, 0.0) + jnp.log(1.0 + jnp.exp(-jnp.abs(t)))
    hy_t = conv3(proj[:, o_hy:o_dt], ch_ref).T
    for j in range(3):
        hyt_ref[0, j] = hy_t[j * HY_CH:(j + 1) * HY_CH]
    a = proj[main, :W_A]
    cq = a[:, :Q_LORA]
    ckv = a[:, Q_LORA:Q_LORA + KV_LORA]
    krb = a[:, Q_LORA + KV_LORA:Q_LORA + KV_LORA + HEAD_PAD]
    cqn = (_rms(cq) * gcq_ref[...]).astype(BF16)
    ckvn = (_rms(ckv) * gckv_ref[...]).astype(BF16)
    qr = _dot(cqn, wuq_ref[...])
    kn = _dot(ckvn, wk_ref[...])
    v_ref[0] = (_dot(ckvn, wv_ref[...]) + vone_ref[...]).astype(BF16)
    gq = gq_ref[...]
    gk = gk_ref[...]
    if use_rope:
        wuqp_ref, gqp_ref, gkp_ref, ct_ref, sn_ref = refs[N_INPROJ_IN:N_INPROJ_IN + 5]
        qp = _dot(cqn, wuqp_ref[...])
        krp = a[:, Q_LORA + KV_LORA + HEAD_PAD:]
        gqp = gqp_ref[...]
        gkp = gkp_ref[...]
        ct = ct_ref[...]
        sn = sn_ref[...]
        gq, gqp, gk, gkp = gq * ct, gqp * sn, gk * ct, gkp * sn

    def head_norm_rope(t, g, tp, gp):
        ss = jnp.sum(t * t, axis=-1, keepdims=True) * (1.0 / D_QK)
        inv = lax.rsqrt(ss + EPS)
        if not use_rope:
            return t * inv * g
        return (t * g + tp * gp) * inv

    for hh in range(N_HEADS):
        sl = slice(HEAD_PAD * hh, HEAD_PAD * (hh + 1))
        q_ref[0, :, sl] = head_norm_rope(qr[:, sl], gq, qp[:, sl] if use_rope else None,
                                         gqp if use_rope else None).astype(BF16)
        k_ref[0, :, sl] = head_norm_rope(kn[:, sl] + krb, gk, krp if use_rope else None,
                                         gkp if use_rope else None).astype(BF16)


def _inproj(x, modtok, mod_row0, sw, layer, rope_tabs, tm, seq_len):
    bsz, ntok, _ = x.shape
    hw = N_HEADS * HEAD_PAD
    r = tm // HALO
    nhalo = ntok // HALO

    def tok(width, dtype):
        return jax.ShapeDtypeStruct((bsz, ntok, width), dtype)

    def tokspec(width):
        return pl.BlockSpec((1, tm, width), lambda b, i: (b, i, 0))

    names = ["g_mix", "wall", "taps_x", "taps_hy", "dt_b", "g_cq", "g_ckv", "wuq", "wk", "wv", "vone", "g_q", "g_k"]
    assert 5 + len(names) == N_INPROJ_IN
    tabs = [] if rope_tabs is None else list(rope_tabs)
    if rope_tabs is not None:
        names += ["wuqp", "g_qp", "g_kp"]
    weights = [sw[n] for n in names]
    tabspec = pl.BlockSpec((tm, HEAD_PAD), lambda b, i: (i, 0))
    return pl.pallas_call(
        functools.partial(_inproj_kernel, use_rope=rope_tabs is not None, seq_len=seq_len),
        out_shape=[tok(hw, BF16), tok(hw, BF16), tok(hw, BF16), tok(S_INNER, F32), tok(S_XBC, F32),
                   tok(LANES, F32), jax.ShapeDtypeStruct((bsz, 3, HY_CH, ntok), F32)],
        grid=(bsz, ntok // tm),
        in_specs=[tokspec(D_MODEL),
                  pl.BlockSpec((1, HALO, D_MODEL), lambda b, i: (b, jnp.maximum(i * r - 1, 0), 0)),
                  pl.BlockSpec((1, HALO, D_MODEL), lambda b, i: (b, jnp.minimum((i + 1) * r, nhalo - 1), 0)),
                  _modspec(mod_row0, 0), _modspec(mod_row0, 1)]
        + [_layer(w, layer) for w in weights] + [tabspec] * len(tabs),
        out_specs=[tokspec(hw), tokspec(hw), tokspec(hw), tokspec(S_INNER), tokspec(S_XBC), tokspec(LANES),
                   pl.BlockSpec((1, 3, HY_CH, tm), lambda b, i: (b, 0, 0, i))],
        compiler_params=_cparams("parallel", "parallel"),
        name="inproj",
    )(x, x, x, modtok, modtok, *weights, *tabs)


def _attn_kernel(*refs, seg_rows):
    q_ref = refs[0]
    o_ref = refs[-1]
    tq = q_ref.shape[1]
    slices = [slice(HEAD_PAD * hh, HEAD_PAD * (hh + 1)) for hh in range(2)]

    def scores(k_ref, start, size):
        return tuple(_dot_nt(q_ref[0, :, sl], k_ref[0, pl.ds(start, size), sl]) for sl in slices)

    def consume(state, s, v_ref, start, size):
        new = []
        for hh in range(2):
            m, acc = state[hh]
            m_new = jnp.maximum(m, jnp.max(s[hh], axis=-1, keepdims=True))
            p = jnp.exp2(s[hh] - m_new)
            acc = jnp.exp2(m - m_new) * acc + _dot(p.astype(BF16), v_ref[0, pl.ds(start, size), slices[hh]])
            new.append((m_new, acc))
        return tuple(new)

    state = tuple((jnp.full((tq, 1), -jnp.inf, F32), jnp.zeros((tq, HEAD_PAD), F32)) for _ in range(2))
    for seg, rows in enumerate(seg_rows):
        k_ref = refs[1 + 2 * seg]
        v_ref = refs[2 + 2 * seg]
        n_full = rows // TK
        if n_full:
            def body(t, st, k_ref=k_ref, v_ref=v_ref):
                start = pl.multiple_of(t * TK, TK)
                return consume(st, scores(k_ref, start, TK), v_ref, start, TK)

            state = lax.fori_loop(0, n_full, body, state, unroll=ATT_UNROLL)
        if rows % TK:
            state = consume(state, scores(k_ref, n_full * TK, rows % TK), v_ref, n_full * TK, rows % TK)
    acc_e = state[0][1]
    acc_o = state[1][1]
    lane = lax.broadcasted_iota(jnp.int32, (tq, HEAD_PAD), 1)
    o_ref[0] = jnp.where(lane < D_V, acc_e / acc_e[:, D_V:D_V + 1], acc_o / acc_o[:, 0:1]).astype(BF16)


def _attention(q, kvs, tq):
    bsz, nq, _ = q.shape
    pw = 2 * HEAD_PAD
    in_specs = [pl.BlockSpec((1, tq, pw), lambda b, p, i: (b, i, p))]
    args = [q]
    for k, v in kvs:
        spec = pl.BlockSpec((1, k.shape[1], pw), lambda b, p, i: (b, 0, p))
        in_specs += [spec, spec]
        args += [k, v]
    return pl.pallas_call(
        functools.partial(_attn_kernel, seg_rows=tuple(k.shape[1] for k, _ in kvs)),
        out_shape=jax.ShapeDtypeStruct((bsz, nq, N_HEADS * D_V), BF16),
        grid=(bsz, N_HEADS // 2, nq // tq),
        in_specs=in_specs,
        out_specs=pl.BlockSpec((1, tq, 2 * D_V), lambda b, p, i: (b, i, p)),
        compiler_params=_cparams("parallel", "parallel", "parallel"),
        name="attention",
    )(*args)


def _split3(x):
    hi = x.astype(BF16)
    r1 = x - hi.astype(F32)
    mid = r1.astype(BF16)
    return hi, mid, (r1 - mid.astype(F32)).astype(BF16)


def _ssd_chunk(u, dt, a_row, st, reverse):
    off = S_HEADS if reverse else 0
    hpg = S_HEADS // S_GROUPS
    xs = u[:, :S_INNER]
    bmat = u[:, S_INNER:S_INNER + LANES]
    cmat = u[:, S_INNER + LANES:]
    ii = lax.broadcasted_iota(jnp.int32, (CHUNK, CHUNK), 0)
    jj = lax.broadcasted_iota(jnp.int32, (CHUNK, CHUNK), 1)
    tri = (jj >= ii) if reverse else (jj <= ii)
    tri16 = tri.astype(BF16)
    cum = sum(_dot(tri16, part) for part in _split3(dt * a_row))
    cum_t = cum.T
    dt_t = dt.T
    tot = cum[0:1] if reverse else cum[CHUNK - 1:CHUNK]
    lane = lax.broadcasted_iota(jnp.int32, (CHUNK, LANES), 1)
    lane2 = lax.broadcasted_iota(jnp.int32, (2 * CHUNK, LANES), 1)
    b16 = bmat.astype(BF16)
    bmat_t = bmat.T
    cms = [jnp.where((lane // S_STATE) == g, cmat, 0.0) for g in range(S_GROUPS)]
    cbs = [jnp.where(tri, _dot_nt(cms[g].astype(BF16), b16), 0.0) for g in range(S_GROUPS)]
    bts = [jnp.where((ii // S_STATE) == g, bmat_t, 0.0) for g in range(S_GROUPS)]
    ys = []
    sts = []
    for pair in range(S_HEADS // 2):
        sl = slice(LANES * pair, LANES * (pair + 1))
        both = jnp.concatenate([xs[:, sl], st[:, sl]], axis=0)
        acc = None
        new = None
        decay = []
        for half in range(2):
            hh = 2 * pair + half
            g = hh // hpg
            c = off + hh
            col = jnp.broadcast_to(cum[:, c:c + 1], (CHUNK, LANES))
            row = cum_t[c:c + 1, :]
            dtr = dt_t[c:c + 1, :]
            dec = jnp.exp(jnp.minimum(col - row, 0.0))
            lhs = jnp.concatenate([cbs[g] * dec * dtr, cms[g] * jnp.exp(col)], axis=1).astype(BF16)
            rhs = jnp.where((lane2 // S_HDIM) == half, both, 0.0).astype(BF16)
            part = _dot(lhs, rhs)
            acc = part if acc is None else acc + part
            tot_h = tot[:, c:c + 1]
            wrow = jnp.exp(tot_h - row) * dtr
            pn = _dot((bts[g] * wrow).astype(BF16), rhs[:CHUNK])
            new = pn if new is None else new + pn
            decay.append(jnp.exp(tot_h))
        ys.append(acc)
        sts.append(st[:, sl] * jnp.where(lane[0:1] < S_HDIM, decay[0], decay[1]) + new)
    return jnp.concatenate(ys, axis=1), jnp.concatenate(sts, axis=1)


def _ssd_kernel(uf_ref, dtf_ref, ub_ref, dtb_ref, a_ref, initf_ref, initb_ref,
                yf_ref, yb_ref, finf_ref, finb_ref, stf_ref, stb_ref, *, nchunks):
    @pl.when(pl.program_id(1) == 0)
    def _():
        stf_ref[...] = initf_ref[...]
        stb_ref[...] = initb_ref[...]

    a_row = a_ref[...]
    for bb in range(uf_ref.shape[0]):
        st = stf_ref[bb]
        for k in range(nchunks):
            rows = slice(k * CHUNK, (k + 1) * CHUNK)
            y, st = _ssd_chunk(uf_ref[bb, rows, :], dtf_ref[bb, rows, :], a_row, st, False)
            yf_ref[bb, rows, :] = y
        stf_ref[bb] = st
        st = stb_ref[bb]
        for k in reversed(range(nchunks)):
            rows = slice(k * CHUNK, (k + 1) * CHUNK)
            y, st = _ssd_chunk(ub_ref[bb, rows, :], dtb_ref[bb, rows, :], a_row, st, True)
            yb_ref[bb, rows, :] = y
        stb_ref[bb] = st

    @pl.when(pl.program_id(1) == pl.num_programs(1) - 1)
    def _():
        finf_ref[...] = stf_ref[...]
        finb_ref[...] = stb_ref[...]


def _ssd(u, dt, sw, layer, init_f, init_b, nchunks, nbat):
    bsz, ntok, width = u.shape
    blk = nchunks * CHUNK
    nsteps = ntok // blk
    srows = S_GROUPS * S_STATE
    fwd = lambda b, s: (b, s, 0)
    bwd = lambda b, s: (b, nsteps - 1 - s, 0)
    state = pl.BlockSpec((nbat, srows, S_INNER), lambda b, s: (b, 0, 0))
    yshape = jax.ShapeDtypeStruct((bsz, ntok, S_INNER), F32)
    sshape = jax.ShapeDtypeStruct((bsz, srows, S_INNER), F32)
    return pl.pallas_call(
        functools.partial(_ssd_kernel, nchunks=nchunks),
        out_shape=[yshape, yshape, sshape, sshape],
        grid=(bsz // nbat, nsteps),
        in_specs=[pl.BlockSpec((nbat, blk, width), fwd), pl.BlockSpec((nbat, blk, LANES), fwd),
                  pl.BlockSpec((nbat, blk, width), bwd), pl.BlockSpec((nbat, blk, LANES), bwd),
                  _layer(sw["a_row"], layer), state, state],
        out_specs=[pl.BlockSpec((nbat, blk, S_INNER), fwd), pl.BlockSpec((nbat, blk, S_INNER), bwd), state, state],
        scratch_shapes=[pltpu.VMEM((nbat, srows, S_INNER), F32), pltpu.VMEM((nbat, srows, S_INNER), F32)],
        compiler_params=_cparams("parallel", "arbitrary"),
        name="ssd",
    )(u, dt, u, dt, sw["a_row"], init_f, init_b)


def _hyfilt_kernel(ft_ref, w1_ref, b1_ref, q1_ref, w2_ref, b2_ref, q2_ref, w3b_ref, w3f_ref, dl_ref, o_ref, h_ref):
    n = ft_ref.shape[1]
    half = n // 2

    @pl.when(pl.program_id(0) == 0)
    def _():
        h = jnp.sin(q1_ref[...] * (_dot(w1_ref[...], ft_ref[...], HIGHEST) + b1_ref[...]))
        h_ref[...] = jnp.sin(q2_ref[...] * (_dot(w2_ref[...], h, HIGHEST) + b2_ref[...]))

    fb = _dot(w3b_ref[...], h_ref[:, :half], HIGHEST)
    ff = _dot(w3f_ref[...], h_ref[:, half:], HIGHEST)
    f = jnp.concatenate([fb, ff], axis=1)
    f = f * jnp.exp(-ft_ref[0:1, :] * dl_ref[...])
    pos = lax.broadcasted_iota(jnp.int32, f.shape, 1)
    f = jnp.where(pos == 0, 0.0, f)
    o_ref[...] = f / (jnp.sum(jnp.abs(f), axis=1, keepdims=True) + EPS)


def _hyfilt(feats_t, sw, layer):
    n = feats_t.shape[1]
    rows = 2 * HY_CH
    rt = 128
    args = [sw[k] for k in ("w1t", "b1", "q1", "w2t", "b2", "q2")]
    rowspec = lambda width: pl.BlockSpec((None, rt, width), lambda r: (layer, r, 0))
    return pl.pallas_call(
        _hyfilt_kernel,
        out_shape=jax.ShapeDtypeStruct((rows, n), F32),
        grid=(rows // rt,),
        in_specs=[_full(feats_t)] + [_layer(a, layer) for a in args]
        + [rowspec(HY_HIDDEN), rowspec(HY_HIDDEN), rowspec(1)],
        out_specs=pl.BlockSpec((rt, n), lambda r: (r, 0)),
        scratch_shapes=[pltpu.VMEM((HY_HIDDEN, n), F32)],
        compiler_params=_cparams("arbitrary"),
        name="hyena_filter",
    )(feats_t, *args, sw["w3b"], sw["w3f"], sw["delta"])


def _hyspec_kernel(f_ref, c_ref, s_ref, g_ref, *, blk, nb):
    kt = pl.program_id(0)
    ftile = c_ref.shape[1]
    ctab = c_ref[...].astype(BF16)
    stab = s_ref[...].astype(BF16)
    freq = lax.broadcasted_iota(jnp.int32, (2 * HY_CH, ftile), 1) + kt * ftile
    sigma = jnp.where((freq & 1) == 0, 1.0, -1.0)
    scale = 2.0 / (2 * blk)
    prev = None
    for e in range(2 * nb):
        phi = f_ref[:, e * blk:(e + 1) * blk]
        p16 = phi.astype(BF16)
        a = _dot(p16, ctab)
        bs = _dot(p16, stab)
        cur = (a, bs, phi[:, 0:1])
        if prev is not None:
            gr = (a + sigma * prev[1]) * scale
            gi = (sigma * (prev[0] - prev[2]) - bs) * scale
            for o in range(2):
                g_ref[o, e - 1, 0] = gr[o * HY_CH:(o + 1) * HY_CH]
                g_ref[o, e - 1, 1] = gi[o * HY_CH:(o + 1) * HY_CH]
        prev = cur


def _hyspec(filt, ctab, stab, blk):
    n = filt.shape[1]
    nb = n // (2 * blk)
    ft = min(HY_SUB, blk)
    return pl.pallas_call(
        functools.partial(_hyspec_kernel, blk=blk, nb=nb),
        out_shape=jax.ShapeDtypeStruct((2, 2 * nb - 1, 2, HY_CH, blk), F32),
        grid=(blk // ft,),
        in_specs=[_full(filt),
                  pl.BlockSpec((blk, ft), lambda k: (0, k)),
                  pl.BlockSpec((blk, ft), lambda k: (0, k))],
        out_specs=pl.BlockSpec((2, 2 * nb - 1, 2, HY_CH, ft), lambda k: (0, 0, 0, 0, k)),
        compiler_params=_cparams("parallel"),
        name="hyena_spectra",
    )(filt, ctab, stab)


def _hyconv_kernel(u_ref, m_ref, d_ref, g_ref, c_ref, s_ref, o_ref, ub_ref, acc_ref, *, blk, nb):
    kt = pl.program_id(1)

    @pl.when(kt == 0)
    def _():
        for j in range(nb):
            ub_ref[j * HY_CH:(j + 1) * HY_CH, :] = u_ref[0, 0, :, j * blk:(j + 1) * blk].astype(BF16)
        acc_ref[...] = jnp.zeros_like(acc_ref)

    ub = ub_ref[...]
    total = None
    for f0 in range(0, c_ref.shape[1], HY_SUB):
        fs = slice(f0, f0 + HY_SUB)
        ctab = c_ref[:, fs]
        stab = s_ref[:, fs]
        xr = _dot(ub, ctab)
        xs = _dot(ub, stab)
        yrs = []
        yss = []
        for i in range(nb):
            yr = None
            ys = None
            for j in range(nb):
                d = i - j + nb - 1
                gr = g_ref[0, d, 0, :, fs]
                gi = g_ref[0, d, 1, :, fs]
                xrj = xr[j * HY_CH:(j + 1) * HY_CH]
                xsj = xs[j * HY_CH:(j + 1) * HY_CH]
                tr = gr * xrj + gi * xsj
                ts = gr * xsj - gi * xrj
                yr = tr if yr is None else yr + tr
                ys = ts if ys is None else ys + ts
            yrs.append(yr.astype(BF16))
            yss.append(ys.astype(BF16))
        part = _dot_nt(jnp.concatenate(yrs, axis=0), ctab) + _dot_nt(jnp.concatenate(yss, axis=0), stab)
        total = part if total is None else total + part
    acc_ref[...] += total

    @pl.when(kt == pl.num_programs(1) - 1)
    def _():
        for i in range(nb):
            cols = slice(i * blk, (i + 1) * blk)
            conv = acc_ref[i * HY_CH:(i + 1) * HY_CH, :]
            o_ref[0, :, cols] = (m_ref[0, 0, :, cols] * (conv + u_ref[0, 0, :, cols] * d_ref[...])).astype(o_ref.dtype)


def _hyconv(u4, usel, m4, msel, dcols, dsel, gspec, order, tabs, blk, out_dtype):
    bsz, _, _, n = u4.shape
    nb = n // blk
    ft = min(HY_FT, blk)
    ctab, stab = tabs
    return pl.pallas_call(
        functools.partial(_hyconv_kernel, blk=blk, nb=nb),
        out_shape=jax.ShapeDtypeStruct((bsz, HY_CH, n), out_dtype),
        grid=(bsz, blk // ft),
        in_specs=[pl.BlockSpec((1, 1, HY_CH, n), lambda b, k: (b, usel, 0, 0)),
                  pl.BlockSpec((1, 1, HY_CH, n), lambda b, k: (b, msel, 0, 0)),
                  pl.BlockSpec((None, HY_CH, 1), lambda b, k: (dsel, 0, 0)),
                  pl.BlockSpec((1, 2 * nb - 1, 2, HY_CH, ft), lambda b, k: (order, 0, 0, 0, k)),
                  pl.BlockSpec((blk, ft), lambda b, k: (0, k)),
                  pl.BlockSpec((blk, ft), lambda b, k: (0, k))],
        out_specs=pl.BlockSpec((1, HY_CH, n), lambda b, k: (b, 0, 0)),
        scratch_shapes=[pltpu.VMEM((nb * HY_CH, blk), BF16), pltpu.VMEM((nb * HY_CH, blk), F32)],
        compiler_params=_cparams("parallel", "arbitrary"),
        name="hyena_conv",
    )(u4, m4, dcols, gspec, ctab, stab)


def _dft_tables(blk):
    s = np.arange(blk, dtype=np.int64)[:, None]
    k = np.arange(blk, dtype=np.int64)[None, :]
    ang = ((s * (2 * k + 1)) % (4 * blk)).astype(np.float64) * (2.0 * math.pi / (4 * blk))
    return jnp.asarray(np.cos(ang), F32), jnp.asarray(np.sin(ang), F32)


def _hyena_features(length):
    p = np.arange(length)
    tb = np.where(p == 0, 0, length - p)
    t = np.concatenate([tb, p]).astype(np.float64)
    t01 = t / (length - 1)
    w = (2.0 * math.pi / length) * t
    bands = np.linspace(1e-4, HY_BANDS - 1, HY_BANDS)[:, None]
    feats = np.concatenate([t01[None, :], np.cos(bands * w[None, :]), -np.sin(bands * w[None, :])], axis=0)
    pad = (-feats.shape[0]) % SUBLANES
    return jnp.asarray(np.pad(feats, ((0, pad), (0, 0))), F32)


def _hyena(p4, sw, layer, tabs, blk, out_dtype):
    bsz, _, _, n = p4.shape
    filt = _hyfilt(_hyena_features(n), sw, layer)
    gspec = _hyspec(filt, tabs[0], tabs[1], blk)
    z1 = _hyconv(p4, 0, p4, 1, sw["d_hy"], 2 * layer, gspec, 0, tabs, blk, F32)
    return _hyconv(z1.reshape(bsz, 1, HY_CH, n), 0, p4, 2, sw["d_hy"], 2 * layer + 1, gspec, 1, tabs, blk, out_dtype)


def _post_kernel(x_ref, att_ref, yf_ref, yb_ref, xs_ref, z_ref, hyo_ref, ga1_ref, sh2_ref, sc2_ref, ga2_ref,
                 dsk_ref, gss_ref, wo_ref, g2_ref, w1_ref, w2_ref, o_ref):
    x = x_ref[0]
    y = yf_ref[0] + yb_ref[0] + xs_ref[0] * dsk_ref[...]
    zz = z_ref[0]
    y = y * (zz * _sigmoid(zz))
    gw = S_INNER // S_GROUPS
    lane = lax.broadcasted_iota(jnp.int32, y.shape, 1)
    first = lane < gw
    y2 = y * y
    s0 = jnp.sum(jnp.where(first, y2, 0.0), axis=-1, keepdims=True)
    s1 = jnp.sum(y2, axis=-1, keepdims=True) - s0
    inv = jnp.where(first, lax.rsqrt(s0 * (1.0 / gw) + EPS), lax.rsqrt(s1 * (1.0 / gw) + EPS))
    ssm = (y * inv * gss_ref[...]).astype(BF16)
    na = N_HEADS * D_V
    mix = _dot(att_ref[0], wo_ref[0:na]) + _dot(ssm, wo_ref[na:na + S_INNER])
    mix = mix + _dot_tn(hyo_ref[0], wo_ref[na + S_INNER:])
    x1 = x + ga1_ref[0] * mix
    h2 = (_rms(x1) * g2_ref[...]) * (1.0 + sc2_ref[0]) + sh2_ref[0]
    hb = h2.astype(BF16)
    acc = jnp.zeros_like(x1)
    fc = 1024
    for c in range(D_FF // fc):
        t = jnp.maximum(_dot(hb, w1_ref[:, c * fc:(c + 1) * fc]), 0.0)
        acc = acc + _dot((t * t).astype(BF16), w2_ref[c * fc:(c + 1) * fc, :])
    o_ref[0] = x1 + ga2_ref[0] * acc


def _post(x, att, yf, yb, u, z, hyo, modtok, mod_row0, sw, layer, tm):
    bsz, ntok, _ = x.shape

    def tokspec(width):
        return pl.BlockSpec((1, tm, width), lambda b, i: (b, i, 0))

    weights = [sw[k] for k in ("dskip", "g_ssm", "wo", "g_mlp", "w1", "w2")]
    return pl.pallas_call(
        _post_kernel,
        out_shape=jax.ShapeDtypeStruct((bsz, ntok, D_MODEL), F32),
        grid=(bsz, ntok // tm),
        in_specs=[tokspec(D_MODEL), tokspec(N_HEADS * D_V), tokspec(S_INNER), tokspec(S_INNER), tokspec(S_INNER),
                  tokspec(S_INNER), pl.BlockSpec((1, HY_CH, tm), lambda b, i: (b, 0, i)),
                  _modspec(mod_row0, 2), _modspec(mod_row0, 3), _modspec(mod_row0, 4), _modspec(mod_row0, 5)]
        + [_layer(w, layer) for w in weights],
        out_specs=tokspec(D_MODEL),
        compiler_params=_cparams("parallel", "parallel"),
        name="post",
    )(x, att, yf, yb, u, z, hyo, modtok, modtok, modtok, modtok, *weights)


def _stacked_weights(p):
    depth = p["w_in"].shape[0]
    o = np.cumsum([0, Q_LORA, KV_LORA, D_ROPE, S_INNER, S_XBC, 2 * S_HEADS, 3 * HY_CH])
    w_in_t = jnp.swapaxes(p["w_in"], 1, 2)
    wcq, wckv, wkr, wz, wx, wdt, why = (w_in_t[:, o[j]:o[j + 1]] for j in range(7))

    def partner(t, axis):
        f = ROPE_FREQS
        return jnp.concatenate([lax.slice_in_dim(t, (j ^ 1) * f, ((j ^ 1) + 1) * f, axis=axis)
                                for j in range(D_ROPE // f)], axis=axis)

    def pad_rows(t, lo, hi):
        return jnp.pad(t, ((0, 0), (lo, hi), (0, 0)))

    wall = jnp.concatenate([wcq, wckv, pad_rows(wkr, D_NOPE, HEAD_PAD - D_QK),
                            pad_rows(partner(wkr, 1), D_NOPE, HEAD_PAD - D_QK), wz, wx, why,
                            pad_rows(wdt, 0, LANES - 2 * S_HEADS)], axis=1).astype(BF16)
    assert wall.shape[1] == W_ALL
    hw = N_HEADS * HEAD_PAD
    headpad = lambda t, lo, hi: jnp.pad(t, ((0, 0), (0, 0), (0, 0), (lo, hi))).reshape(depth, t.shape[1], hw)
    wuq4 = p["w_uq"].reshape(depth, Q_LORA, N_HEADS, D_QK)
    wukv = p["w_ukv"].reshape(depth, KV_LORA, N_HEADS, D_NOPE + D_V)
    odd = (np.arange(N_HEADS) % 2 == 1)[None, None, :, None]
    wv4 = jnp.where(odd, jnp.pad(wukv[..., D_NOPE:], ((0, 0), (0, 0), (0, 0), (HEAD_PAD - D_V, 0))),
                    jnp.pad(wukv[..., D_NOPE:], ((0, 0), (0, 0), (0, 0), (0, HEAD_PAD - D_V))))
    vone = np.zeros((N_HEADS, HEAD_PAD), np.float32)
    vone[0::2, D_V] = 1.0
    vone[1::2, 0] = 1.0
    row = lambda t: t.reshape(depth, 1, -1)
    lanepad = lambda t, lo, hi: jnp.pad(t, ((0, 0), (0, 0), (lo, hi)))
    taps = lambda w, b: jnp.pad(jnp.concatenate([w, b[:, None, :]], axis=1), ((0, 0), (0, SUBLANES - 4), (0, 0)))
    col = lambda t: t.reshape(depth, -1, 1)
    qscale = math.log2(math.e) / math.sqrt(D_QK)
    delta = np.abs(np.linspace(math.log(HY_DECAY_TARGET) / HY_DECAY_PCT_LONG,
                               math.log(HY_DECAY_TARGET) / HY_DECAY_PCT_SHORT, HY_CH))
    return dict(
        g_mix=row(p["g_norm_mix"]), wall=wall,
        taps_x=taps(p["w_conv_ssm"], p["b_conv_ssm"]), taps_hy=taps(p["w_conv_hy"], p["b_conv_hy"]),
        dt_b=lanepad(row(p["dt_bias"]), 0, LANES - 2 * S_HEADS),
        g_cq=row(p["g_cq"]), g_ckv=row(p["g_ckv"]),
        wuq=headpad(wuq4, 0, HEAD_PAD - D_QK).astype(BF16),
        wuqp=headpad(partner(wuq4[..., D_NOPE:], 3), D_NOPE, HEAD_PAD - D_QK).astype(BF16),
        wk=headpad(wukv[..., :D_NOPE], 0, HEAD_PAD - D_NOPE).astype(BF16),
        wv=wv4.reshape(depth, KV_LORA, hw).astype(BF16),
        vone=jnp.asarray(np.tile(vone.reshape(1, 1, hw), (depth, 1, 1))),
        g_q=lanepad(row(p["g_qhead"]), 0, HEAD_PAD - D_QK) * qscale, g_k=lanepad(row(p["g_khead"]), 0, HEAD_PAD - D_QK),
        g_qp=lanepad(partner(row(p["g_qhead"])[..., D_NOPE:], 2), D_NOPE, HEAD_PAD - D_QK) * qscale,
        g_kp=lanepad(partner(row(p["g_khead"])[..., D_NOPE:], 2), D_NOPE, HEAD_PAD - D_QK),
        a_row=lanepad(row(-jnp.exp(p["a_log"].astype(F32))), 0, LANES - 2 * S_HEADS),
        dskip=row(jnp.repeat(p["d_skip_ssm"], S_HDIM, axis=1)), g_ssm=row(p["g_ssm_out"]),
        wo=p["w_out"].astype(BF16), g_mlp=row(p["g_norm_mlp"]),
        w1=p["w_ff1"].astype(BF16), w2=p["w_ff2"].astype(BF16),
        w1t=lanepad(jnp.swapaxes(p["w_f1"], 1, 2), 0, (-HY_EMB) % SUBLANES),
        b1=col(p["b_f1"]), q1=col(p["freq_f1"]), w2t=jnp.swapaxes(p["w_f2"], 1, 2),
        b2=col(p["b_f2"]), q2=col(p["freq_f2"]),
        w3f=jnp.swapaxes(p["w_f3"][:, :, :2 * HY_CH], 1, 2), w3b=jnp.swapaxes(p["w_f3"][:, :, 2 * HY_CH:], 1, 2),
        delta=jnp.asarray(np.tile(delta, (depth, 2)).reshape(depth, 2 * HY_CH, 1), F32),
        d_hy=p["d_skip_hy"].reshape(depth * 2, HY_CH, 1),
    )


def _rope_tables(seq):
    pos = np.arange(seq)
    inv = ROPE_THETA ** (-np.arange(ROPE_FREQS, dtype=np.float64) / ROPE_FREQS)
    ang = np.stack([(pos // GRID_W)[:, None] * inv, (pos % GRID_W)[:, None] * inv], axis=1)
    cos, sin = np.cos(ang), np.sin(ang)
    ct = np.ones((seq, HEAD_PAD))
    sn = np.zeros((seq, HEAD_PAD))
    for axis in range(2):
        lo = D_NOPE + axis * 2 * ROPE_FREQS
        mid = lo + ROPE_FREQS
        ct[:, lo:mid] = cos[:, axis]
        ct[:, mid:mid + ROPE_FREQS] = cos[:, axis]
        sn[:, lo:mid] = -sin[:, axis]
        sn[:, mid:mid + ROPE_FREQS] = sin[:, axis]
    return jnp.asarray(ct, F32), jnp.asarray(sn, F32)


def kernel(x, c, ctx, c_ctx, w_mod, b_mod, g_norm_mix, g_norm_mlp, w_in, w_out, g_cq, g_ckv, w_uq, w_ukv, g_qhead, g_khead, w_conv_ssm, b_conv_ssm, a_log, dt_bias, d_skip_ssm, g_ssm_out, w_conv_hy, b_conv_hy, w_f1, b_f1, freq_f1, w_f2, b_f2, freq_f2, w_f3, d_skip_hy, w_ff1, w_ff2):
    params = dict(w_in=w_in, w_out=w_out, g_norm_mix=g_norm_mix, g_norm_mlp=g_norm_mlp, g_cq=g_cq, g_ckv=g_ckv,
                  w_uq=w_uq, w_ukv=w_ukv, g_qhead=g_qhead, g_khead=g_khead, a_log=a_log, d_skip_ssm=d_skip_ssm,
                  g_ssm_out=g_ssm_out, w_conv_hy=w_conv_hy, b_conv_hy=b_conv_hy, w_f1=w_f1, b_f1=b_f1,
                  freq_f1=freq_f1, w_f2=w_f2, b_f2=b_f2, freq_f2=freq_f2, w_f3=w_f3, d_skip_hy=d_skip_hy,
                  w_ff1=w_ff1, w_ff2=w_ff2, w_conv_ssm=w_conv_ssm, b_conv_ssm=b_conv_ssm, dt_bias=dt_bias)
    bsz, seq, _ = x.shape
    nctx = ctx.shape[1]
    assert seq % HY_BLOCK == 0 and seq % TM == 0 and (bsz * nctx) % TM == 0 and TM % nctx == 0
    assert bsz + 1 <= SUBLANES and CHUNK == S_GROUPS * S_STATE and nctx % CHUNK == 0

    cvec = jnp.pad(jnp.concatenate([c, c_ctx[None, :]], axis=0), ((0, SUBLANES - bsz - 1), (0, 0)))
    mod = _modulation(cvec, w_mod, b_mod)
    modtok = mod.reshape(DEPTH * SUBLANES, 1, 6 * D_MODEL)

    sw = _stacked_weights(params)
    rope_tabs = _rope_tables(seq)
    tabs_lat = tuple(t.astype(BF16) for t in _dft_tables(HY_BLOCK))
    tabs_ctx = tuple(t.astype(BF16) for t in _dft_tables(nctx))
    zero_state = jnp.zeros((bsz, S_GROUPS * S_STATE, S_INNER), F32)

    xl = x
    xc = ctx.reshape(1, bsz * nctx, D_MODEL)
    for i in range(DEPTH):
        last = i == DEPTH - 1
        row_l = i * SUBLANES
        row_c = i * SUBLANES + bsz

        q_c, k_c, v_c, z_c, u_c, dt_c, hyt_c = _inproj(xc, modtok, row_c, sw, i, None, TM, nctx)
        per_b = lambda t: t.reshape(bsz, nctx, t.shape[-1])
        k_c, v_c, u_c, dt_c = per_b(k_c), per_b(v_c), per_b(u_c), per_b(dt_c)
        yf_c, yb_c, s_fwd, s_bwd = _ssd(u_c, dt_c, sw, i, zero_state, zero_state, nctx // CHUNK, SSD_BATCH)

        q, k, v, z, u, dt, hyt = _inproj(xl, modtok, row_l, sw, i, rope_tabs, TM, seq)
        att = _attention(q, [(k, v), (k_c, v_c)], TQ)
        yf, yb, _, _ = _ssd(u, dt, sw, i, s_fwd, s_bwd, SSD_GROUP, SSD_BATCH)
        hyo = _hyena(hyt, sw, i, tabs_lat, HY_BLOCK, BF16)
        xl = _post(xl, att, yf, yb, u, z, hyo, modtok, row_l, sw, i, TM)
        if last:
            return xl

        att_c = _attention(per_b(q_c), [(k_c, v_c)], nctx)
        hyt_cb = hyt_c.reshape(3, HY_CH, bsz, nctx).transpose(2, 0, 1, 3)
        hyo_c = _hyena(hyt_cb, sw, i, tabs_ctx, nctx, BF16)
        flat = lambda t: t.reshape(1, bsz * nctx, t.shape[-1])
        hyo_cf = hyo_c.transpose(1, 0, 2).reshape(1, HY_CH, bsz * nctx)
        xc = _post(xc, flat(att_c), flat(yf_c), flat(yb_c), flat(u_c), z_c, hyo_cf, modtok, row_c, sw, i, TM)
```

```python
import functools
import math

import jax
import jax.numpy as jnp
import numpy as np
from jax import lax
from jax.experimental import pallas as pl
from jax.experimental.pallas import tpu as pltpu

F32 = jnp.float32
BF16 = jnp.bfloat16
HIGHEST = lax.Precision.HIGHEST

D_MODEL = 1024
DEPTH = 2
GRID_W = 64
EPS = 1e-6
N_HEADS = 6
D_NOPE = 64
D_ROPE = 32
D_QK = D_NOPE + D_ROPE
D_V = 64
Q_LORA = 256
KV_LORA = 128
ROPE_THETA = 10000.0
ROPE_FREQS = D_ROPE // 4
S_HEADS = 6
S_HDIM = 64
S_INNER = S_HEADS * S_HDIM
S_GROUPS = 2
S_STATE = 64
S_XBC = S_INNER + 2 * S_GROUPS * S_STATE
HY_CH = D_MODEL - N_HEADS * D_V - S_INNER
HY_BANDS = 16
HY_EMB = 1 + 2 * HY_BANDS
HY_HIDDEN = 64
HY_DECAY_PCT_SHORT = 0.3
HY_DECAY_PCT_LONG = 1.5
HY_DECAY_TARGET = 1e-2
D_FF = 4 * D_MODEL

LANES = 128
SUBLANES = 8
VMEM_LIMIT = 56 * 1024 * 1024

TM = 512
HALO = 16
MOD_TN = 1024
FF_CHUNK = 1024
TQ = 1024
TK = 512
ATT_UNROLL = 8
HEAD_PAD = LANES
CHUNK = 128
SSD_GROUP = 4
SSD_BATCH = 2
HY_BLOCK = 1024
HY_FT = 256
HY_SUB = 256
HY_CHAINS = 2
HY_FILT_ROWS = 128

N_INPROJ_IN = 18
W_A = Q_LORA + KV_LORA + 2 * HEAD_PAD
W_ALL = W_A + S_INNER + S_XBC + 3 * HY_CH + LANES


def _cparams(*sem):
    return pltpu.CompilerParams(dimension_semantics=sem, vmem_limit_bytes=VMEM_LIMIT)


def _dot(a, b, precision=None):
    return jnp.dot(a, b, preferred_element_type=F32, precision=precision)


def _dot_nt(a, b):
    return lax.dot_general(a, b, (((1,), (1,)), ((), ())), preferred_element_type=F32)


def _dot_tn(a, b):
    return lax.dot_general(a, b, (((0,), (0,)), ((), ())), preferred_element_type=F32)


def _rms(x):
    return x * lax.rsqrt(jnp.mean(x * x, axis=-1, keepdims=True) + EPS)


def _sigmoid(x):
    return 1.0 / (1.0 + jnp.exp(-x))


def _full(arr):
    return pl.BlockSpec(arr.shape, lambda *_: (0,) * arr.ndim, pipeline_mode=pl.Buffered(1))


def _layer(arr, layer):
    return pl.BlockSpec((None,) + arr.shape[1:], lambda *_: (layer,) + (0,) * (arr.ndim - 1),
                        pipeline_mode=pl.Buffered(1))


def _mod_kernel(c_ref, w_ref, b_ref, o_ref):
    cv = c_ref[...]
    s = (cv * _sigmoid(cv)).astype(BF16)
    o_ref[0] = _dot(s, w_ref[0].astype(BF16)) + b_ref[0]


def _modulation(cvec, w_mod, b_mod):
    tn = MOD_TN
    ncol = w_mod.shape[-1]
    return pl.pallas_call(
        _mod_kernel,
        out_shape=jax.ShapeDtypeStruct((DEPTH, SUBLANES, ncol), F32),
        grid=(DEPTH, ncol // tn),
        in_specs=[pl.BlockSpec((SUBLANES, D_MODEL), lambda l, j: (0, 0)),
                  pl.BlockSpec((1, D_MODEL, tn), lambda l, j: (l, 0, j)),
                  pl.BlockSpec((1, 1, tn), lambda l, j: (l, 0, j))],
        out_specs=pl.BlockSpec((1, SUBLANES, tn), lambda l, j: (l, 0, j)),
        compiler_params=_cparams("parallel", "parallel"),
        name="modulation",
    )(cvec, w_mod, b_mod.reshape(DEPTH, 1, ncol))


def _modspec(row0, chunk):
    return pl.BlockSpec((1, 1, D_MODEL), lambda b, i: (row0 + b, 0, chunk))


def _inproj_kernel(*refs, use_rope, seq_len):
    (x_ref, xp_ref, xn_ref, sh_ref, sc_ref, g_ref, wall_ref, cx_ref, ch_ref,
     dtb_ref, gcq_ref, gckv_ref, wuq_ref, wk_ref, wv_ref, vone_ref, gq_ref, gk_ref) = refs[:N_INPROJ_IN]
    q_ref, k_ref, v_ref, z_ref, u_ref, dt_ref, hyt_ref = refs[-7:]
    tm = x_ref.shape[1]

    gain = g_ref[...] * (1.0 + sc_ref[0])

    def normmod(xv):
        return (_rms(xv) * gain + sh_ref[0]).astype(BF16)

    i = pl.program_id(1)
    hp = normmod(xp_ref[0])
    hn = normmod(xn_ref[0])
    edges_only = seq_len % tm == 0
    if edges_only:
        tiles_per_seq = seq_len // tm
        hp = jnp.where(i % tiles_per_seq == 0, jnp.zeros_like(hp), hp)
        hn = jnp.where(i % tiles_per_seq == tiles_per_seq - 1, jnp.zeros_like(hn), hn)
    else:
        pos = (i * tm + lax.broadcasted_iota(jnp.int32, (tm, 1), 0)) % seq_len
        has_prev = pos != 0
        has_next = pos != seq_len - 1
    hb_ext = jnp.concatenate([hp, normmod(x_ref[0]), hn], axis=0)
    proj = _dot_nt(hb_ext, wall_ref[...])
    main = slice(HALO, HALO + tm)

    def conv3(ext, taps_ref):
        n = ext.shape[0]
        up = pltpu.roll(ext, 1, 0)[main]
        dn = pltpu.roll(ext, n - 1, 0)[main]
        if not edges_only:
            up = jnp.where(has_prev, up, 0.0)
            dn = jnp.where(has_next, dn, 0.0)
        taps = taps_ref[...]
        return up * taps[0:1] + ext[main] * taps[1:2] + dn * taps[2:3] + taps[3:4]

    o_z = W_A
    o_x = o_z + S_INNER
    o_hy = o_x + S_XBC
    o_dt = o_hy + 3 * HY_CH
    z_ref[0] = proj[main, o_z:o_x]
    xc = conv3(proj[:, o_x:o_hy], cx_ref)
    u_ref[0] = xc * _sigmoid(xc)
    t = proj[main, o_dt:] + dtb_ref[...]
    dt_ref[0] = jnp.maximum(t, 0.0) + jnp.log(1.0 + jnp.exp(-jnp.abs(t)))
    hy_t = conv3(proj[:, o_hy:o_dt], ch_ref).T
    for j in range(3):
        hyt_ref[0, j] = hy_t[j * HY_CH:(j + 1) * HY_CH]
    a = proj[main, :W_A]
    cq = a[:, :Q_LORA]
    ckv = a[:, Q_LORA:Q_LORA + KV_LORA]
    krb = a[:, Q_LORA + KV_LORA:Q_LORA + KV_LORA + HEAD_PAD]
    cqn = (_rms(cq) * gcq_ref[...]).astype(BF16)
    ckvn = (_rms(ckv) * gckv_ref[...]).astype(BF16)
    qr = _dot(cqn, wuq_ref[...])
    kn = _dot(ckvn, wk_ref[...])
    v_ref[0] = (_dot(ckvn, wv_ref[...]) + vone_ref[...]).astype(BF16)
    gq = gq_ref[...]
    gk = gk_ref[...]
    if use_rope:
        wuqp_ref, gqp_ref, gkp_ref, ct_ref, sn_ref = refs[N_INPROJ_IN:N_INPROJ_IN + 5]
        qp = _dot(cqn, wuqp_ref[...])
        krp = a[:, Q_LORA + KV_LORA + HEAD_PAD:]
        gqp = gqp_ref[...]
        gkp = gkp_ref[...]
        ct = ct_ref[...]
        sn = sn_ref[...]
        gq, gqp, gk, gkp = gq * ct, gqp * sn, gk * ct, gkp * sn

    def head_norm_rope(t, g, tp, gp):
        ss = jnp.sum(t * t, axis=-1, keepdims=True) * (1.0 / D_QK)
        inv = lax.rsqrt(ss + EPS)
        if not use_rope:
            return t * inv * g
        return (t * g + tp * gp) * inv

    for hh in range(N_HEADS):
        sl = slice(HEAD_PAD * hh, HEAD_PAD * (hh + 1))
        q_ref[0, :, sl] = head_norm_rope(qr[:, sl], gq, qp[:, sl] if use_rope else None,
                                         gqp if use_rope else None).astype(BF16)
        k_ref[0, :, sl] = head_norm_rope(kn[:, sl] + krb, gk, krp if use_rope else None,
                                         gkp if use_rope else None).astype(BF16)


def _inproj(x, modtok, mod_row0, sw, layer, rope_tabs, tm, seq_len):
    bsz, ntok, _ = x.shape
    hw = N_HEADS * HEAD_PAD
    r = tm // HALO
    nhalo = ntok // HALO

    def tok(width, dtype):
        return jax.ShapeDtypeStruct((bsz, ntok, width), dtype)

    def tokspec(width):
        return pl.BlockSpec((1, tm, width), lambda b, i: (b, i, 0))

    names = ["g_mix", "wall", "taps_x", "taps_hy", "dt_b", "g_cq", "g_ckv", "wuq", "wk", "wv", "vone", "g_q", "g_k"]
    assert 5 + len(names) == N_INPROJ_IN
    tabs = [] if rope_tabs is None else list(rope_tabs)
    if rope_tabs is not None:
        names += ["wuqp", "g_qp", "g_kp"]
    weights = [sw[n] for n in names]
    tabspec = pl.BlockSpec((tm, HEAD_PAD), lambda b, i: (i, 0))
    return pl.pallas_call(
        functools.partial(_inproj_kernel, use_rope=rope_tabs is not None, seq_len=seq_len),
        out_shape=[tok(hw, BF16), tok(hw, BF16), tok(hw, BF16), tok(S_INNER, F32), tok(S_XBC, F32),
                   tok(LANES, F32), jax.ShapeDtypeStruct((bsz, 3, HY_CH, ntok), F32)],
        grid=(bsz, ntok // tm),
        in_specs=[tokspec(D_MODEL),
                  pl.BlockSpec((1, HALO, D_MODEL), lambda b, i: (b, jnp.maximum(i * r - 1, 0), 0)),
                  pl.BlockSpec((1, HALO, D_MODEL), lambda b, i: (b, jnp.minimum((i + 1) * r, nhalo - 1), 0)),
                  _modspec(mod_row0, 0), _modspec(mod_row0, 1)]
        + [_layer(w, layer) for w in weights] + [tabspec] * len(tabs),
        out_specs=[tokspec(hw), tokspec(hw), tokspec(hw), tokspec(S_INNER), tokspec(S_XBC), tokspec(LANES),
                   pl.BlockSpec((1, 3, HY_CH, tm), lambda b, i: (b, 0, 0, i))],
        compiler_params=_cparams("parallel", "parallel"),
        name="inproj",
    )(x, x, x, modtok, modtok, *weights, *tabs)


def _attn_kernel(*refs, seg_rows):
    q_ref = refs[0]
    o_ref = refs[-1]
    tq = q_ref.shape[1]
    slices = [slice(HEAD_PAD * hh, HEAD_PAD * (hh + 1)) for hh in range(2)]

    def scores(k_ref, start, size):
        return tuple(_dot_nt(q_ref[0, :, sl], k_ref[0, pl.ds(start, size), sl]) for sl in slices)

    def consume(state, s, v_ref, start, size):
        new = []
        for hh in range(2):
            m, acc = state[hh]
            m_new = jnp.maximum(m, jnp.max(s[hh], axis=-1, keepdims=True))
            p = jnp.exp2(s[hh] - m_new)
            acc = jnp.exp2(m - m_new) * acc + _dot(p.astype(BF16), v_ref[0, pl.ds(start, size), slices[hh]])
            new.append((m_new, acc))
        return tuple(new)

    state = tuple((jnp.full((tq, 1), -jnp.inf, F32), jnp.zeros((tq, HEAD_PAD), F32)) for _ in range(2))
    for seg, rows in enumerate(seg_rows):
        k_ref = refs[1 + 2 * seg]
        v_ref = refs[2 + 2 * seg]
        n_full = rows // TK
        if n_full:
            def body(t, st, k_ref=k_ref, v_ref=v_ref):
                start = pl.multiple_of(t * TK, TK)
                return consume(st, scores(k_ref, start, TK), v_ref, start, TK)

            state = lax.fori_loop(0, n_full, body, state, unroll=ATT_UNROLL)
        if rows % TK:
            state = consume(state, scores(k_ref, n_full * TK, rows % TK), v_ref, n_full * TK, rows % TK)
    acc_e = state[0][1]
    acc_o = state[1][1]
    lane = lax.broadcasted_iota(jnp.int32, (tq, HEAD_PAD), 1)
    o_ref[0] = jnp.where(lane < D_V, acc_e / acc_e[:, D_V:D_V + 1], acc_o / acc_o[:, 0:1]).astype(BF16)


def _attention(q, kvs, tq):
    bsz, nq, _ = q.shape
    pw = 2 * HEAD_PAD
    in_specs = [pl.BlockSpec((1, tq, pw), lambda b, p, i: (b, i, p))]
    args = [q]
    for k, v in kvs:
        spec = pl.BlockSpec((1, k.shape[1], pw), lambda b, p, i: (b, 0, p))
        in_specs += [spec, spec]
        args += [k, v]
    return pl.pallas_call(
        functools.partial(_attn_kernel, seg_rows=tuple(k.shape[1] for k, _ in kvs)),
        out_shape=jax.ShapeDtypeStruct((bsz, nq, N_HEADS * D_V), BF16),
        grid=(bsz, N_HEADS // 2, nq // tq),
        in_specs=in_specs,
        out_specs=pl.BlockSpec((1, tq, 2 * D_V), lambda b, p, i: (b, i, p)),
        compiler_params=_cparams("parallel", "parallel", "parallel"),
        name="attention",
    )(*args)


def _split3(x):
    hi = x.astype(BF16)
    r1 = x - hi.astype(F32)
    mid = r1.astype(BF16)
    return hi, mid, (r1 - mid.astype(F32)).astype(BF16)


def _ssd_prepare(problems, a_row):
    hpg = S_HEADS // S_GROUPS
    ii = lax.broadcasted_iota(jnp.int32, (CHUNK, CHUNK), 0)
    jj = lax.broadcasted_iota(jnp.int32, (CHUNK, CHUNK), 1)
    lane = lax.broadcasted_iota(jnp.int32, (CHUNK, LANES), 1)
    tris = {False: jj <= ii, True: jj >= ii}
    tri16 = {r: t.astype(BF16) for r, t in tris.items()}
    splits = [_split3(dt * a_row) for _, dt, _ in problems]
    cums = [sum(_dot(tri16[rev], part) for part in sp) for sp, (_, _, rev) in zip(splits, problems)]
    cum_ts = [c.T for c in cums]
    dt_ts = [dt.T for _, dt, _ in problems]
    bmats = [u[:, S_INNER:S_INNER + LANES] for u, _, _ in problems]
    bmat_ts = [b.T for b in bmats]
    cms = [[jnp.where((lane // S_STATE) == g, u[:, S_INNER + LANES:], 0.0) for g in range(S_GROUPS)]
           for u, _, _ in problems]
    cbs = [[jnp.where(tris[rev], _dot_nt(cm[g].astype(BF16), b.astype(BF16)), 0.0) for g in range(S_GROUPS)]
           for cm, b, (_, _, rev) in zip(cms, bmats, problems)]
    bts = [[jnp.where((ii // S_STATE) == g, bt, 0.0) for g in range(S_GROUPS)] for bt in bmat_ts]
    out = []
    for p, (u, _, rev) in enumerate(problems):
        off = S_HEADS if rev else 0
        tot = cums[p][0:1] if rev else cums[p][CHUNK - 1:CHUNK]
        lhs = []
        wst = []
        decay = []
        for hh in range(S_HEADS):
            g = hh // hpg
            c = off + hh
            col = jnp.broadcast_to(cums[p][:, c:c + 1], (CHUNK, LANES))
            row = cum_ts[p][c:c + 1, :]
            dtr = dt_ts[p][c:c + 1, :]
            dec = jnp.exp(jnp.minimum(col - row, 0.0))
            lhs.append(jnp.concatenate([cbs[p][g] * dec * dtr, cms[p][g] * jnp.exp(col)], axis=1).astype(BF16))
            tot_h = tot[:, c:c + 1]
            wst.append((bts[p][g] * (jnp.exp(tot_h - row) * dtr)).astype(BF16))
            decay.append(jnp.exp(tot_h))
        out.append((u[:, :S_INNER], lhs, wst, decay))
    return out


def _ssd_apply(prep, st):
    xs, lhs, wst, decay = prep
    lane = lax.broadcasted_iota(jnp.int32, (1, LANES), 1)
    lane2 = lax.broadcasted_iota(jnp.int32, (2 * CHUNK, LANES), 1)
    ys = []
    sts = []
    for pair in range(S_HEADS // 2):
        sl = slice(LANES * pair, LANES * (pair + 1))
        both = jnp.concatenate([xs[:, sl], st[:, sl]], axis=0)
        acc = None
        new = None
        for half in range(2):
            hh = 2 * pair + half
            rhs = jnp.where((lane2 // S_HDIM) == half, both, 0.0).astype(BF16)
            part = _dot(lhs[hh], rhs)
            acc = part if acc is None else acc + part
            pn = _dot(wst[hh], rhs[:CHUNK])
            new = pn if new is None else new + pn
        ys.append(acc)
        sts.append(st[:, sl] * jnp.where(lane < S_HDIM, decay[2 * pair], decay[2 * pair + 1]) + new)
    return jnp.concatenate(ys, axis=1), jnp.concatenate(sts, axis=1)


def _ssd_kernel(uf_ref, dtf_ref, ub_ref, dtb_ref, a_ref, initf_ref, initb_ref,
                yf_ref, yb_ref, finf_ref, finb_ref, stf_ref, stb_ref, *, nchunks):
    @pl.when(pl.program_id(1) == 0)
    def _():
        stf_ref[...] = initf_ref[...]
        stb_ref[...] = initb_ref[...]

    nbat = uf_ref.shape[0]
    chains = [(bb, rev) for bb in range(nbat) for rev in (False, True)]
    order = {False: list(range(nchunks)), True: list(reversed(range(nchunks)))}
    problems = []
    for bb, rev in chains:
        u_ref, d_ref = (ub_ref, dtb_ref) if rev else (uf_ref, dtf_ref)
        for k in order[rev]:
            rows = slice(k * CHUNK, (k + 1) * CHUNK)
            problems.append((u_ref[bb, rows, :], d_ref[bb, rows, :], rev))
    preps = _ssd_prepare(problems, a_ref[...])
    states = [(stb_ref if rev else stf_ref)[bb] for bb, rev in chains]
    for step in range(nchunks):
        for c, (bb, rev) in enumerate(chains):
            y, states[c] = _ssd_apply(preps[c * nchunks + step], states[c])
            rows = slice(order[rev][step] * CHUNK, (order[rev][step] + 1) * CHUNK)
            (yb_ref if rev else yf_ref)[bb, rows, :] = y
    for c, (bb, rev) in enumerate(chains):
        (stb_ref if rev else stf_ref)[bb] = states[c]

    @pl.when(pl.program_id(1) == pl.num_programs(1) - 1)
    def _():
        finf_ref[...] = stf_ref[...]
        finb_ref[...] = stb_ref[...]


def _ssd(u, dt, sw, layer, init_f, init_b, nchunks, nbat):
    bsz, ntok, width = u.shape
    blk = nchunks * CHUNK
    nsteps = ntok // blk
    srows = S_GROUPS * S_STATE
    fwd = lambda b, s: (b, s, 0)
    bwd = lambda b, s: (b, nsteps - 1 - s, 0)
    state = pl.BlockSpec((nbat, srows, S_INNER), lambda b, s: (b, 0, 0))
    yshape = jax.ShapeDtypeStruct((bsz, ntok, S_INNER), F32)
    sshape = jax.ShapeDtypeStruct((bsz, srows, S_INNER), F32)
    return pl.pallas_call(
        functools.partial(_ssd_kernel, nchunks=nchunks),
        out_shape=[yshape, yshape, sshape, sshape],
        grid=(bsz // nbat, nsteps),
        in_specs=[pl.BlockSpec((nbat, blk, width), fwd), pl.BlockSpec((nbat, blk, LANES), fwd),
                  pl.BlockSpec((nbat, blk, width), bwd), pl.BlockSpec((nbat, blk, LANES), bwd),
                  _layer(sw["a_row"], layer), state, state],
        out_specs=[pl.BlockSpec((nbat, blk, S_INNER), fwd), pl.BlockSpec((nbat, blk, S_INNER), bwd), state, state],
        scratch_shapes=[pltpu.VMEM((nbat, srows, S_INNER), F32), pltpu.VMEM((nbat, srows, S_INNER), F32)],
        compiler_params=_cparams("parallel", "arbitrary"),
        name="ssd",
    )(u, dt, u, dt, sw["a_row"], init_f, init_b)


def _hyfilt_kernel(ft_ref, w1_ref, b1_ref, q1_ref, w2_ref, b2_ref, q2_ref, w3b_ref, w3f_ref, dl_ref, o_ref, h_ref):
    n = ft_ref.shape[1]
    half = n // 2

    @pl.when(pl.program_id(0) == 0)
    def _():
        h = jnp.sin(q1_ref[...] * (_dot(w1_ref[...], ft_ref[:, half:], HIGHEST) + b1_ref[...]))
        hf = jnp.sin(q2_ref[...] * (_dot(w2_ref[...], h, HIGHEST) + b2_ref[...]))
        h_ref[:, half:] = hf
        ii = lax.broadcasted_iota(jnp.int32, (LANES, LANES), 0)
        jj = lax.broadcasted_iota(jnp.int32, (LANES, LANES), 1)
        exchange = (ii + jj == LANES - 1).astype(F32)
        nblk = half // LANES
        rev = jnp.concatenate([_dot(hf[:, (nblk - 1 - b) * LANES:(nblk - b) * LANES], exchange, HIGHEST)
                               for b in range(nblk)], axis=1)
        h_ref[:, :half] = pltpu.roll(rev, 1, 1)

    fb = _dot(w3b_ref[...], h_ref[:, :half], HIGHEST)
    ff = _dot(w3f_ref[...], h_ref[:, half:], HIGHEST)
    f = jnp.concatenate([fb, ff], axis=1)
    f = f * jnp.exp(-ft_ref[0:1, :] * dl_ref[...])
    pos = lax.broadcasted_iota(jnp.int32, f.shape, 1)
    f = jnp.where(pos == 0, 0.0, f)
    o_ref[...] = f / (jnp.sum(jnp.abs(f), axis=1, keepdims=True) + EPS)


def _hyfilt(feats_t, sw, layer):
    n = feats_t.shape[1]
    rows = 2 * HY_CH
    rt = HY_FILT_ROWS
    args = [sw[k] for k in ("w1t", "b1", "q1", "w2t", "b2", "q2")]
    rowspec = lambda width: pl.BlockSpec((None, rt, width), lambda r: (layer, r, 0))
    return pl.pallas_call(
        _hyfilt_kernel,
        out_shape=jax.ShapeDtypeStruct((rows, n), F32),
        grid=(rows // rt,),
        in_specs=[_full(feats_t)] + [_layer(a, layer) for a in args]
        + [rowspec(HY_HIDDEN), rowspec(HY_HIDDEN), rowspec(1)],
        out_specs=pl.BlockSpec((rt, n), lambda r: (r, 0)),
        scratch_shapes=[pltpu.VMEM((HY_HIDDEN, n), F32)],
        compiler_params=_cparams("arbitrary"),
        name="hyena_filter",
    )(feats_t, *args, sw["w3b"], sw["w3f"], sw["delta"])


def _hyspec_kernel(f_ref, c_ref, s_ref, g_ref, *, blk, nb):
    kt = pl.program_id(0)
    ftile = c_ref.shape[1]
    ctab = c_ref[...].astype(BF16)
    stab = s_ref[...].astype(BF16)
    freq = lax.broadcasted_iota(jnp.int32, (2 * HY_CH, ftile), 1) + kt * ftile
    sigma = jnp.where((freq & 1) == 0, 1.0, -1.0)
    scale = 2.0 / (2 * blk)
    prev = None
    for e in range(2 * nb):
        phi = f_ref[:, e * blk:(e + 1) * blk]
        p16 = phi.astype(BF16)
        a = _dot(p16, ctab)
        bs = _dot(p16, stab)
        cur = (a, bs, phi[:, 0:1])
        if prev is not None:
            gr = (a + sigma * prev[1]) * scale
            gi = (sigma * (prev[0] - prev[2]) - bs) * scale
            for o in range(2):
                g_ref[o, e - 1, 0] = gr[o * HY_CH:(o + 1) * HY_CH]
                g_ref[o, e - 1, 1] = gi[o * HY_CH:(o + 1) * HY_CH]
        prev = cur


def _hyspec(filt, ctab, stab, blk):
    n = filt.shape[1]
    nb = n // (2 * blk)
    ft = min(HY_SUB, blk)
    return pl.pallas_call(
        functools.partial(_hyspec_kernel, blk=blk, nb=nb),
        out_shape=jax.ShapeDtypeStruct((2, 2 * nb - 1, 2, HY_CH, blk), F32),
        grid=(blk // ft,),
        in_specs=[_full(filt),
                  pl.BlockSpec((blk, ft), lambda k: (0, k)),
                  pl.BlockSpec((blk, ft), lambda k: (0, k))],
        out_specs=pl.BlockSpec((2, 2 * nb - 1, 2, HY_CH, ft), lambda k: (0, 0, 0, 0, k)),
        compiler_params=_cparams("parallel"),
        name="hyena_spectra",
    )(filt, ctab, stab)


def _hyconv_kernel(u_ref, m_ref, d_ref, g_ref, c_ref, s_ref, o_ref, ub_ref, yr_ref, ys_ref, z_ref, *, blk, nb):
    order = pl.program_id(1)
    kt = pl.program_id(2)
    ft = g_ref.shape[-1]
    cg = HY_CH // HY_CHAINS
    rows_per_chain = nb * cg

    @pl.when(jnp.logical_and(order == 0, kt == 0))
    def _():
        z_ref[...] = u_ref[0, 0]

    @pl.when(kt == 0)
    def _():
        for h in range(HY_CHAINS):
            for j in range(nb):
                r0 = h * rows_per_chain + j * cg
                ub_ref[r0:r0 + cg, :] = z_ref[h * cg:(h + 1) * cg, j * blk:(j + 1) * blk].astype(BF16)

    fs = pl.ds(pl.multiple_of(kt * ft, ft), ft)
    ctab = c_ref[:, fs]
    stab = s_ref[:, fs]
    spectra = []
    for h in range(HY_CHAINS):
        ub = ub_ref[h * rows_per_chain:(h + 1) * rows_per_chain, :]
        spectra.append((_dot(ub, ctab), _dot(ub, stab)))
    for h, (xr, xs) in enumerate(spectra):
        ch = slice(h * cg, (h + 1) * cg)
        for i in range(nb):
            yr = None
            ys = None
            for j in range(nb):
                d = i - j + nb - 1
                gr = g_ref[0, d, 0, ch, :]
                gi = g_ref[0, d, 1, ch, :]
                xrj = xr[j * cg:(j + 1) * cg]
                xsj = xs[j * cg:(j + 1) * cg]
                tr = gr * xrj + gi * xsj
                ts = gr * xsj - gi * xrj
                yr = tr if yr is None else yr + tr
                ys = ts if ys is None else ys + ts
            r0 = h * rows_per_chain + i * cg
            yr_ref[r0:r0 + cg, fs] = yr.astype(BF16)
            ys_ref[r0:r0 + cg, fs] = ys.astype(BF16)

    @pl.when(kt == pl.num_programs(2) - 1)
    def _():
        for h in range(HY_CHAINS):
            ch = slice(h * cg, (h + 1) * cg)
            rows = slice(h * rows_per_chain, (h + 1) * rows_per_chain)
            conv = _dot_nt(yr_ref[rows, :], c_ref[...]) + _dot_nt(ys_ref[rows, :], s_ref[...])
            for i in range(nb):
                cols = slice(i * blk, (i + 1) * blk)
                val = m_ref[0, 0, ch, cols] * (conv[i * cg:(i + 1) * cg] + z_ref[ch, cols] * d_ref[ch, :])
                z_ref[ch, cols] = val
                o_ref[0, ch, cols] = val.astype(o_ref.dtype)


def _hyconv(p4, dcols, layer, gspec, tabs, blk, out_dtype):
    bsz, _, _, n = p4.shape
    nb = n // blk
    ft = min(HY_FT, blk)
    ctab, stab = tabs
    return pl.pallas_call(
        functools.partial(_hyconv_kernel, blk=blk, nb=nb),
        out_shape=jax.ShapeDtypeStruct((bsz, HY_CH, n), out_dtype),
        grid=(bsz, 2, blk // ft),
        in_specs=[pl.BlockSpec((1, 1, HY_CH, n), lambda b, o, k: (b, 0, 0, 0)),
                  pl.BlockSpec((1, 1, HY_CH, n), lambda b, o, k: (b, o + 1, 0, 0)),
                  pl.BlockSpec((None, HY_CH, 1), lambda b, o, k: (2 * layer + o, 0, 0)),
                  pl.BlockSpec((1, 2 * nb - 1, 2, HY_CH, ft), lambda b, o, k: (o, 0, 0, 0, k)),
                  _full(ctab), _full(stab)],
        out_specs=pl.BlockSpec((1, HY_CH, n), lambda b, o, k: (b, 0, 0)),
        scratch_shapes=[pltpu.VMEM((nb * HY_CH, blk), BF16)] * 3 + [pltpu.VMEM((HY_CH, n), F32)],
        compiler_params=_cparams("parallel", "arbitrary", "arbitrary"),
        name="hyena_conv",
    )(p4, p4, dcols, gspec, ctab, stab)


def _dft_tables(blk):
    s = np.arange(blk, dtype=np.int64)[:, None]
    k = np.arange(blk, dtype=np.int64)[None, :]
    ang = ((s * (2 * k + 1)) % (4 * blk)).astype(np.float64) * (2.0 * math.pi / (4 * blk))
    return jnp.asarray(np.cos(ang), F32), jnp.asarray(np.sin(ang), F32)


def _hyena_features(length):
    p = np.arange(length)
    tb = np.where(p == 0, 0, length - p)
    t = np.concatenate([tb, p]).astype(np.float64)
    t01 = t / (length - 1)
    w = (2.0 * math.pi / length) * t
    bands = np.linspace(1e-4, HY_BANDS - 1, HY_BANDS)[:, None]
    feats = np.concatenate([t01[None, :], np.cos(bands * w[None, :]), -np.sin(bands * w[None, :])], axis=0)
    pad = (-feats.shape[0]) % SUBLANES
    return jnp.asarray(np.pad(feats, ((0, pad), (0, 0))), F32)


def _hyena(p4, sw, layer, tabs, blk, out_dtype):
    bsz, _, _, n = p4.shape
    filt = _hyfilt(_hyena_features(n), sw, layer)
    gspec = _hyspec(filt, tabs[0], tabs[1], blk)
    return _hyconv(p4, sw["d_hy"], layer, gspec, tabs, blk, out_dtype)


def _post_kernel(x_ref, att_ref, yf_ref, yb_ref, xs_ref, z_ref, hyo_ref, ga1_ref, sh2_ref, sc2_ref, ga2_ref,
                 dsk_ref, gss_ref, wo_ref, g2_ref, w1_ref, w2_ref, o_ref):
    x = x_ref[0]
    y = yf_ref[0] + yb_ref[0] + xs_ref[0] * dsk_ref[...]
    zz = z_ref[0]
    y = y * (zz * _sigmoid(zz))
    gw = S_INNER // S_GROUPS
    lane = lax.broadcasted_iota(jnp.int32, y.shape, 1)
    first = lane < gw
    y2 = y * y
    s0 = jnp.sum(jnp.where(first, y2, 0.0), axis=-1, keepdims=True)
    s1 = jnp.sum(y2, axis=-1, keepdims=True) - s0
    inv = jnp.where(first, lax.rsqrt(s0 * (1.0 / gw) + EPS), lax.rsqrt(s1 * (1.0 / gw) + EPS))
    ssm = (y * inv * gss_ref[...]).astype(BF16)
    na = N_HEADS * D_V
    mix = _dot(att_ref[0], wo_ref[0:na]) + _dot(ssm, wo_ref[na:na + S_INNER])
    mix = mix + _dot_tn(hyo_ref[0], wo_ref[na + S_INNER:])
    x1 = x + ga1_ref[0] * mix
    h2 = (_rms(x1) * g2_ref[...]) * (1.0 + sc2_ref[0]) + sh2_ref[0]
    hb = h2.astype(BF16)
    acc = jnp.zeros_like(x1)
    fc = FF_CHUNK
    for c in range(D_FF // fc):
        t = jnp.maximum(_dot(hb, w1_ref[:, c * fc:(c + 1) * fc]), 0.0)
        acc = acc + _dot((t * t).astype(BF16), w2_ref[c * fc:(c + 1) * fc, :])
    o_ref[0] = x1 + ga2_ref[0] * acc


def _post(x, att, yf, yb, u, z, hyo, modtok, mod_row0, sw, layer, tm):
    bsz, ntok, _ = x.shape

    def tokspec(width):
        return pl.BlockSpec((1, tm, width), lambda b, i: (b, i, 0))

    weights = [sw[k] for k in ("dskip", "g_ssm", "wo", "g_mlp", "w1", "w2")]
    return pl.pallas_call(
        _post_kernel,
        out_shape=jax.ShapeDtypeStruct((bsz, ntok, D_MODEL), F32),
        grid=(bsz, ntok // tm),
        in_specs=[tokspec(D_MODEL), tokspec(N_HEADS * D_V), tokspec(S_INNER), tokspec(S_INNER), tokspec(S_INNER),
                  tokspec(S_INNER), pl.BlockSpec((1, HY_CH, tm), lambda b, i: (b, 0, i)),
                  _modspec(mod_row0, 2), _modspec(mod_row0, 3), _modspec(mod_row0, 4), _modspec(mod_row0, 5)]
        + [_layer(w, layer) for w in weights],
        out_specs=tokspec(D_MODEL),
        compiler_params=_cparams("parallel", "parallel"),
        name="post",
    )(x, att, yf, yb, u, z, hyo, modtok, modtok, modtok, modtok, *weights)


def _stacked_weights(p):
    depth = p["w_in"].shape[0]
    o = np.cumsum([0, Q_LORA, KV_LORA, D_ROPE, S_INNER, S_XBC, 2 * S_HEADS, 3 * HY_CH])
    w_in_t = jnp.swapaxes(p["w_in"], 1, 2)
    wcq, wckv, wkr, wz, wx, wdt, why = (w_in_t[:, o[j]:o[j + 1]] for j in range(7))

    def partner(t, axis):
        f = ROPE_FREQS
        return jnp.concatenate([lax.slice_in_dim(t, (j ^ 1) * f, ((j ^ 1) + 1) * f, axis=axis)
                                for j in range(D_ROPE // f)], axis=axis)

    def pad_rows(t, lo, hi):
        return jnp.pad(t, ((0, 0), (lo, hi), (0, 0)))

    wall = jnp.concatenate([wcq, wckv, pad_rows(wkr, D_NOPE, HEAD_PAD - D_QK),
                            pad_rows(partner(wkr, 1), D_NOPE, HEAD_PAD - D_QK), wz, wx, why,
                            pad_rows(wdt, 0, LANES - 2 * S_HEADS)], axis=1).astype(BF16)
    assert wall.shape[1] == W_ALL
    hw = N_HEADS * HEAD_PAD
    headpad = lambda t, lo, hi: jnp.pad(t, ((0, 0), (0, 0), (0, 0), (lo, hi))).reshape(depth, t.shape[1], hw)
    wuq4 = p["w_uq"].reshape(depth, Q_LORA, N_HEADS, D_QK)
    wukv = p["w_ukv"].reshape(depth, KV_LORA, N_HEADS, D_NOPE + D_V)
    odd = (np.arange(N_HEADS) % 2 == 1)[None, None, :, None]
    wv4 = jnp.where(odd, jnp.pad(wukv[..., D_NOPE:], ((0, 0), (0, 0), (0, 0), (HEAD_PAD - D_V, 0))),
                    jnp.pad(wukv[..., D_NOPE:], ((0, 0), (0, 0), (0, 0), (0, HEAD_PAD - D_V))))
    vone = np.zeros((N_HEADS, HEAD_PAD), np.float32)
    vone[0::2, D_V] = 1.0
    vone[1::2, 0] = 1.0
    row = lambda t: t.reshape(depth, 1, -1)
    lanepad = lambda t, lo, hi: jnp.pad(t, ((0, 0), (0, 0), (lo, hi)))
    taps = lambda w, b: jnp.pad(jnp.concatenate([w, b[:, None, :]], axis=1), ((0, 0), (0, SUBLANES - 4), (0, 0)))
    col = lambda t: t.reshape(depth, -1, 1)
    qscale = math.log2(math.e) / math.sqrt(D_QK)
    delta = np.abs(np.linspace(math.log(HY_DECAY_TARGET) / HY_DECAY_PCT_LONG,
                               math.log(HY_DECAY_TARGET) / HY_DECAY_PCT_SHORT, HY_CH))
    return dict(
        g_mix=row(p["g_norm_mix"]), wall=wall,
        taps_x=taps(p["w_conv_ssm"], p["b_conv_ssm"]), taps_hy=taps(p["w_conv_hy"], p["b_conv_hy"]),
        dt_b=lanepad(row(p["dt_bias"]), 0, LANES - 2 * S_HEADS),
        g_cq=row(p["g_cq"]), g_ckv=row(p["g_ckv"]),
        wuq=headpad(wuq4, 0, HEAD_PAD - D_QK).astype(BF16),
        wuqp=headpad(partner(wuq4[..., D_NOPE:], 3), D_NOPE, HEAD_PAD - D_QK).astype(BF16),
        wk=headpad(wukv[..., :D_NOPE], 0, HEAD_PAD - D_NOPE).astype(BF16),
        wv=wv4.reshape(depth, KV_LORA, hw).astype(BF16),
        vone=jnp.asarray(np.tile(vone.reshape(1, 1, hw), (depth, 1, 1))),
        g_q=lanepad(row(p["g_qhead"]), 0, HEAD_PAD - D_QK) * qscale, g_k=lanepad(row(p["g_khead"]), 0, HEAD_PAD - D_QK),
        g_qp=lanepad(partner(row(p["g_qhead"])[..., D_NOPE:], 2), D_NOPE, HEAD_PAD - D_QK) * qscale,
        g_kp=lanepad(partner(row(p["g_khead"])[..., D_NOPE:], 2), D_NOPE, HEAD_PAD - D_QK),
        a_row=lanepad(row(-jnp.exp(p["a_log"].astype(F32))), 0, LANES - 2 * S_HEADS),
        dskip=row(jnp.repeat(p["d_skip_ssm"], S_HDIM, axis=1)), g_ssm=row(p["g_ssm_out"]),
        wo=p["w_out"].astype(BF16), g_mlp=row(p["g_norm_mlp"]),
        w1=p["w_ff1"].astype(BF16), w2=p["w_ff2"].astype(BF16),
        w1t=lanepad(jnp.swapaxes(p["w_f1"], 1, 2), 0, (-HY_EMB) % SUBLANES),
        b1=col(p["b_f1"]), q1=col(p["freq_f1"]), w2t=jnp.swapaxes(p["w_f2"], 1, 2),
        b2=col(p["b_f2"]), q2=col(p["freq_f2"]),
        w3f=jnp.swapaxes(p["w_f3"][:, :, :2 * HY_CH], 1, 2), w3b=jnp.swapaxes(p["w_f3"][:, :, 2 * HY_CH:], 1, 2),
        delta=jnp.asarray(np.tile(delta, (depth, 2)).reshape(depth, 2 * HY_CH, 1), F32),
        d_hy=p["d_skip_hy"].reshape(depth * 2, HY_CH, 1),
    )


def _rope_tables(seq):
    pos = np.arange(seq)
    inv = ROPE_THETA ** (-np.arange(ROPE_FREQS, dtype=np.float64) / ROPE_FREQS)
    ang = np.stack([(pos // GRID_W)[:, None] * inv, (pos % GRID_W)[:, None] * inv], axis=1)
    cos, sin = np.cos(ang), np.sin(ang)
    ct = np.ones((seq, HEAD_PAD))
    sn = np.zeros((seq, HEAD_PAD))
    for axis in range(2):
        lo = D_NOPE + axis * 2 * ROPE_FREQS
        mid = lo + ROPE_FREQS
        ct[:, lo:mid] = cos[:, axis]
        ct[:, mid:mid + ROPE_FREQS] = cos[:, axis]
        sn[:, lo:mid] = -sin[:, axis]
        sn[:, mid:mid + ROPE_FREQS] = sin[:, axis]
    return jnp.asarray(ct, F32), jnp.asarray(sn, F32)


def kernel(x, c, ctx, c_ctx, w_mod, b_mod, g_norm_mix, g_norm_mlp, w_in, w_out, g_cq, g_ckv, w_uq, w_ukv, g_qhead, g_khead, w_conv_ssm, b_conv_ssm, a_log, dt_bias, d_skip_ssm, g_ssm_out, w_conv_hy, b_conv_hy, w_f1, b_f1, freq_f1, w_f2, b_f2, freq_f2, w_f3, d_skip_hy, w_ff1, w_ff2):
    params = dict(w_in=w_in, w_out=w_out, g_norm_mix=g_norm_mix, g_norm_mlp=g_norm_mlp, g_cq=g_cq, g_ckv=g_ckv,
                  w_uq=w_uq, w_ukv=w_ukv, g_qhead=g_qhead, g_khead=g_khead, a_log=a_log, d_skip_ssm=d_skip_ssm,
                  g_ssm_out=g_ssm_out, w_conv_hy=w_conv_hy, b_conv_hy=b_conv_hy, w_f1=w_f1, b_f1=b_f1,
                  freq_f1=freq_f1, w_f2=w_f2, b_f2=b_f2, freq_f2=freq_f2, w_f3=w_f3, d_skip_hy=d_skip_hy,
                  w_ff1=w_ff1, w_ff2=w_ff2, w_conv_ssm=w_conv_ssm, b_conv_ssm=b_conv_ssm, dt_bias=dt_bias)
    bsz, seq, _ = x.shape
    nctx = ctx.shape[1]
    assert seq % HY_BLOCK == 0 and seq % TM == 0 and (bsz * nctx) % TM == 0 and TM % nctx == 0
    assert bsz + 1 <= SUBLANES and CHUNK == S_GROUPS * S_STATE and nctx % CHUNK == 0

    cvec = jnp.pad(jnp.concatenate([c, c_ctx[None, :]], axis=0), ((0, SUBLANES - bsz - 1), (0, 0)))
    mod = _modulation(cvec, w_mod, b_mod)
    modtok = mod.reshape(DEPTH * SUBLANES, 1, 6 * D_MODEL)

    sw = _stacked_weights(params)
    rope_tabs = _rope_tables(seq)
    tabs_lat = tuple(t.astype(BF16) for t in _dft_tables(HY_BLOCK))
    tabs_ctx = tuple(t.astype(BF16) for t in _dft_tables(nctx))
    zero_state = jnp.zeros((bsz, S_GROUPS * S_STATE, S_INNER), F32)

    xl = x
    xc = ctx.reshape(1, bsz * nctx, D_MODEL)
    for i in range(DEPTH):
        last = i == DEPTH - 1
        row_l = i * SUBLANES
        row_c = i * SUBLANES + bsz

        q_c, k_c, v_c, z_c, u_c, dt_c, hyt_c = _inproj(xc, modtok, row_c, sw, i, None, TM, nctx)
        per_b = lambda t: t.reshape(bsz, nctx, t.shape[-1])
        k_c, v_c, u_c, dt_c = per_b(k_c), per_b(v_c), per_b(u_c), per_b(dt_c)
        yf_c, yb_c, s_fwd, s_bwd = _ssd(u_c, dt_c, sw, i, zero_state, zero_state, nctx // CHUNK, SSD_BATCH)

        q, k, v, z, u, dt, hyt = _inproj(xl, modtok, row_l, sw, i, rope_tabs, TM, seq)
        att = _attention(q, [(k, v), (k_c, v_c)], TQ)
        yf, yb, _, _ = _ssd(u, dt, sw, i, s_fwd, s_bwd, SSD_GROUP, SSD_BATCH)
        hyo = _hyena(hyt, sw, i, tabs_lat, HY_BLOCK, BF16)
        xl = _post(xl, att, yf, yb, u, z, hyo, modtok, row_l, sw, i, TM)
        if last:
            return xl

        att_c = _attention(per_b(q_c), [(k_c, v_c)], nctx)
        hyt_cb = hyt_c.reshape(3, HY_CH, bsz, nctx).transpose(2, 0, 1, 3)
        hyo_c = _hyena(hyt_cb, sw, i, tabs_ctx, nctx, BF16)
        flat = lambda t: t.reshape(1, bsz * nctx, t.shape[-1])
        hyo_cf = hyo_c.transpose(1, 0, 2).reshape(1, HY_CH, bsz * nctx)
        xc = _post(xc, flat(att_c), flat(yf_c), flat(yb_c), flat(u_c), z_c, hyo_cf, modtok, row_c, sw, i, TM)
```

```python
import functools
import math

import jax
import jax.numpy as jnp
import numpy as np
from jax import lax
from jax.experimental import pallas as pl
from jax.experimental.pallas import tpu as pltpu

F32 = jnp.float32
BF16 = jnp.bfloat16
HIGHEST = lax.Precision.HIGHEST

D_MODEL = 1024
DEPTH = 2
GRID_W = 64
EPS = 1e-6
N_HEADS = 6
D_NOPE = 64
D_ROPE = 32
D_QK = D_NOPE + D_ROPE
D_V = 64
Q_LORA = 256
KV_LORA = 128
ROPE_THETA = 10000.0
ROPE_FREQS = D_ROPE // 4
S_HEADS = 6
S_HDIM = 64
S_INNER = S_HEADS * S_HDIM
S_GROUPS = 2
S_STATE = 64
S_XBC = S_INNER + 2 * S_GROUPS * S_STATE
HY_CH = D_MODEL - N_HEADS * D_V - S_INNER
HY_BANDS = 16
HY_EMB = 1 + 2 * HY_BANDS
HY_HIDDEN = 64
HY_DECAY_PCT_SHORT = 0.3
HY_DECAY_PCT_LONG = 1.5
HY_DECAY_TARGET = 1e-2
D_FF = 4 * D_MODEL

LANES = 128
SUBLANES = 8
VMEM_LIMIT = 56 * 1024 * 1024

TM = 512
HALO = 16
MOD_TN = 1024
FF_CHUNK = 1024
TQ = 1024
TK = 512
ATT_UNROLL = 8
HEAD_PAD = LANES
CHUNK = 128
SSD_GROUP = 4
SSD_BATCH = 2
HY_BLOCK = 1024
HY_FT = 256
HY_SUB = 256
HY_CHAINS = 2
HY_FILT_ROWS = 128

N_INPROJ_IN = 18
W_A = Q_LORA + KV_LORA + 2 * HEAD_PAD
W_ALL = W_A + S_INNER + S_XBC + 3 * HY_CH + LANES


def _cparams(*sem):
    return pltpu.CompilerParams(dimension_semantics=sem, vmem_limit_bytes=VMEM_LIMIT)


def _dot(a, b, precision=None):
    return jnp.dot(a, b, preferred_element_type=F32, precision=precision)


def _dot_nt(a, b):
    return lax.dot_general(a, b, (((1,), (1,)), ((), ())), preferred_element_type=F32)


def _dot_tn(a, b):
    return lax.dot_general(a, b, (((0,), (0,)), ((), ())), preferred_element_type=F32)


def _rms(x):
    return x * lax.rsqrt(jnp.mean(x * x, axis=-1, keepdims=True) + EPS)


def _sigmoid(x):
    return 1.0 / (1.0 + jnp.exp(-x))


def _full(arr):
    return pl.BlockSpec(arr.shape, lambda *_: (0,) * arr.ndim, pipeline_mode=pl.Buffered(1))


def _layer(arr, layer):
    return pl.BlockSpec((None,) + arr.shape[1:], lambda *_: (layer,) + (0,) * (arr.ndim - 1),
                        pipeline_mode=pl.Buffered(1))


def _mod_kernel(c_ref, w_ref, b_ref, o_ref):
    cv = c_ref[...]
    s = (cv * _sigmoid(cv)).astype(BF16)
    o_ref[0] = _dot(s, w_ref[0].astype(BF16)) + b_ref[0]


def _modulation(cvec, w_mod, b_mod):
    tn = MOD_TN
    ncol = w_mod.shape[-1]
    return pl.pallas_call(
        _mod_kernel,
        out_shape=jax.ShapeDtypeStruct((DEPTH, SUBLANES, ncol), F32),
        grid=(DEPTH, ncol // tn),
        in_specs=[pl.BlockSpec((SUBLANES, D_MODEL), lambda l, j: (0, 0)),
                  pl.BlockSpec((1, D_MODEL, tn), lambda l, j: (l, 0, j)),
                  pl.BlockSpec((1, 1, tn), lambda l, j: (l, 0, j))],
        out_specs=pl.BlockSpec((1, SUBLANES, tn), lambda l, j: (l, 0, j)),
        compiler_params=_cparams("parallel", "parallel"),
        name="modulation",
    )(cvec, w_mod, b_mod.reshape(DEPTH, 1, ncol))


def _modspec(row0, chunk):
    return pl.BlockSpec((1, 1, D_MODEL), lambda b, i: (row0 + b, 0, chunk))


def _inproj_kernel(*refs, use_rope, seq_len):
    (x_ref, xp_ref, xn_ref, sh_ref, sc_ref, g_ref, wall_ref, cx_ref, ch_ref,
     dtb_ref, gcq_ref, gckv_ref, wuq_ref, wk_ref, wv_ref, vone_ref, gq_ref, gk_ref) = refs[:N_INPROJ_IN]
    q_ref, k_ref, v_ref, z_ref, u_ref, dt_ref, hyt_ref = refs[-7:]
    tm = x_ref.shape[1]

    gain = g_ref[...] * (1.0 + sc_ref[0])

    def normmod(xv):
        return (_rms(xv) * gain + sh_ref[0]).astype(BF16)

    i = pl.program_id(1)
    hp = normmod(xp_ref[0])
    hn = normmod(xn_ref[0])
    edges_only = seq_len % tm == 0
    if edges_only:
        tiles_per_seq = seq_len // tm
        hp = jnp.where(i % tiles_per_seq == 0, jnp.zeros_like(hp), hp)
        hn = jnp.where(i % tiles_per_seq == tiles_per_seq - 1, jnp.zeros_like(hn), hn)
    else:
        pos = (i * tm + lax.broadcasted_iota(jnp.int32, (tm, 1), 0)) % seq_len
        has_prev = pos != 0
        has_next = pos != seq_len - 1
    hb_ext = jnp.concatenate([hp, normmod(x_ref[0]), hn], axis=0)
    proj = _dot_nt(hb_ext, wall_ref[...])
    main = slice(HALO, HALO + tm)

    def conv3(ext, taps_ref):
        n = ext.shape[0]
        up = pltpu.roll(ext, 1, 0)[main]
        dn = pltpu.roll(ext, n - 1, 0)[main]
        if not edges_only:
            up = jnp.where(has_prev, up, 0.0)
            dn = jnp.where(has_next, dn, 0.0)
        taps = taps_ref[...]
        return up * taps[0:1] + ext[main] * taps[1:2] + dn * taps[2:3] + taps[3:4]

    o_z = W_A
    o_x = o_z + S_INNER
    o_hy = o_x + S_XBC
    o_dt = o_hy + 3 * HY_CH
    z_ref[0] = proj[main, o_z:o_x]
    xc = conv3(proj[:, o_x:o_hy], cx_ref)
    u_ref[0] = xc * _sigmoid(xc)
    t = proj[main, o_dt:] + dtb_ref[...]
    dt_ref[0] = jnp.maximum(t, 0.0) + jnp.log(1.0 + jnp.exp(-jnp.abs(t)))
    hy_t = conv3(proj[:, o_hy:o_dt], ch_ref).T
    for j in range(3):
        hyt_ref[0, j] = hy_t[j * HY_CH:(j + 1) * HY_CH]
    a = proj[main, :W_A]
    cq = a[:, :Q_LORA]
    ckv = a[:, Q_LORA:Q_LORA + KV_LORA]
    krb = a[:, Q_LORA + KV_LORA:Q_LORA + KV_LORA + HEAD_PAD]
    cqn = (_rms(cq) * gcq_ref[...]).astype(BF16)
    ckvn = (_rms(ckv) * gckv_ref[...]).astype(BF16)
    qr = _dot(cqn, wuq_ref[...])
    kn = _dot(ckvn, wk_ref[...])
    v_ref[0] = (_dot(ckvn, wv_ref[...]) + vone_ref[...]).astype(BF16)
    gq = gq_ref[...]
    gk = gk_ref[...]
    if use_rope:
        wuqp_ref, gqp_ref, gkp_ref, ct_ref, sn_ref = refs[N_INPROJ_IN:N_INPROJ_IN + 5]
        qp = _dot(cqn, wuqp_ref[...])
        krp = a[:, Q_LORA + KV_LORA + HEAD_PAD:]
        gqp = gqp_ref[...]
        gkp = gkp_ref[...]
        ct = ct_ref[...]
        sn = sn_ref[...]
        gq, gqp, gk, gkp = gq * ct, gqp * sn, gk * ct, gkp * sn

    def head_norm_rope(t, g, tp, gp):
        ss = jnp.sum(t * t, axis=-1, keepdims=True) * (1.0 / D_QK)
        inv = lax.rsqrt(ss + EPS)
        if not use_rope:
            return t * inv * g
        return (t * g + tp * gp) * inv

    for hh in range(N_HEADS):
        sl = slice(HEAD_PAD * hh, HEAD_PAD * (hh + 1))
        q_ref[0, :, sl] = head_norm_rope(qr[:, sl], gq, qp[:, sl] if use_rope else None,
                                         gqp if use_rope else None).astype(BF16)
        k_ref[0, :, sl] = head_norm_rope(kn[:, sl] + krb, gk, krp if use_rope else None,
                                         gkp if use_rope else None).astype(BF16)


def _inproj(x, modtok, mod_row0, sw, layer, rope_tabs, tm, seq_len):
    bsz, ntok, _ = x.shape
    hw = N_HEADS * HEAD_PAD
    r = tm // HALO
    nhalo = ntok // HALO

    def tok(width, dtype):
        return jax.ShapeDtypeStruct((bsz, ntok, width), dtype)

    def tokspec(width):
        return pl.BlockSpec((1, tm, width), lambda b, i: (b, i, 0))

    names = ["g_mix", "wall", "taps_x", "taps_hy", "dt_b", "g_cq", "g_ckv", "wuq", "wk", "wv", "vone", "g_q", "g_k"]
    assert 5 + len(names) == N_INPROJ_IN
    tabs = [] if rope_tabs is None else list(rope_tabs)
    if rope_tabs is not None:
        names += ["wuqp", "g_qp", "g_kp"]
    weights = [sw[n] for n in names]
    tabspec = pl.BlockSpec((tm, HEAD_PAD), lambda b, i: (i, 0))
    return pl.pallas_call(
        functools.partial(_inproj_kernel, use_rope=rope_tabs is not None, seq_len=seq_len),
        out_shape=[tok(hw, BF16), tok(hw, BF16), tok(hw, BF16), tok(S_INNER, F32), tok(S_XBC, F32),
                   tok(LANES, F32), jax.ShapeDtypeStruct((bsz, 3, HY_CH, ntok), F32)],
        grid=(bsz, ntok // tm),
        in_specs=[tokspec(D_MODEL),
                  pl.BlockSpec((1, HALO, D_MODEL), lambda b, i: (b, jnp.maximum(i * r - 1, 0), 0)),
                  pl.BlockSpec((1, HALO, D_MODEL), lambda b, i: (b, jnp.minimum((i + 1) * r, nhalo - 1), 0)),
                  _modspec(mod_row0, 0), _modspec(mod_row0, 1)]
        + [_layer(w, layer) for w in weights] + [tabspec] * len(tabs),
        out_specs=[tokspec(hw), tokspec(hw), tokspec(hw), tokspec(S_INNER), tokspec(S_XBC), tokspec(LANES),
                   pl.BlockSpec((1, 3, HY_CH, tm), lambda b, i: (b, 0, 0, i))],
        compiler_params=_cparams("parallel", "parallel"),
        name="inproj",
    )(x, x, x, modtok, modtok, *weights, *tabs)


def _attn_kernel(*refs, seg_rows):
    q_ref = refs[0]
    o_ref = refs[-1]
    tq = q_ref.shape[1]
    slices = [slice(HEAD_PAD * hh, HEAD_PAD * (hh + 1)) for hh in range(2)]

    def scores(k_ref, start, size):
        return tuple(_dot_nt(q_ref[0, :, sl], k_ref[0, pl.ds(start, size), sl]) for sl in slices)

    def consume(state, s, v_ref, start, size):
        new = []
        for hh in range(2):
            m, acc = state[hh]
            m_new = jnp.maximum(m, jnp.max(s[hh], axis=-1, keepdims=True))
            p = jnp.exp2(s[hh] - m_new)
            acc = jnp.exp2(m - m_new) * acc + _dot(p.astype(BF16), v_ref[0, pl.ds(start, size), slices[hh]])
            new.append((m_new, acc))
        return tuple(new)

    state = tuple((jnp.full((tq, 1), -jnp.inf, F32), jnp.zeros((tq, HEAD_PAD), F32)) for _ in range(2))
    for seg, rows in enumerate(seg_rows):
        k_ref = refs[1 + 2 * seg]
        v_ref = refs[2 + 2 * seg]
        n_full = rows // TK
        if n_full:
            def body(t, st, k_ref=k_ref, v_ref=v_ref):
                start = pl.multiple_of(t * TK, TK)
                return consume(st, scores(k_ref, start, TK), v_ref, start, TK)

            state = lax.fori_loop(0, n_full, body, state, unroll=ATT_UNROLL)
        if rows % TK:
            state = consume(state, scores(k_ref, n_full * TK, rows % TK), v_ref, n_full * TK, rows % TK)
    acc_e = state[0][1]
    acc_o = state[1][1]
    lane = lax.broadcasted_iota(jnp.int32, (tq, HEAD_PAD), 1)
    o_ref[0] = jnp.where(lane < D_V, acc_e / acc_e[:, D_V:D_V + 1], acc_o / acc_o[:, 0:1]).astype(BF16)


def _attention(q, kvs, tq):
    bsz, nq, _ = q.shape
    pw = 2 * HEAD_PAD
    in_specs = [pl.BlockSpec((1, tq, pw), lambda b, p, i: (b, i, p))]
    args = [q]
    for k, v in kvs:
        spec = pl.BlockSpec((1, k.shape[1], pw), lambda b, p, i: (b, 0, p))
        in_specs += [spec, spec]
        args += [k, v]
    return pl.pallas_call(
        functools.partial(_attn_kernel, seg_rows=tuple(k.shape[1] for k, _ in kvs)),
        out_shape=jax.ShapeDtypeStruct((bsz, nq, N_HEADS * D_V), BF16),
        grid=(bsz, N_HEADS // 2, nq // tq),
        in_specs=in_specs,
        out_specs=pl.BlockSpec((1, tq, 2 * D_V), lambda b, p, i: (b, i, p)),
        compiler_params=_cparams("parallel", "parallel", "parallel"),
        name="attention",
    )(*args)


def _split3(x):
    hi = x.astype(BF16)
    r1 = x - hi.astype(F32)
    mid = r1.astype(BF16)
    return hi, mid, (r1 - mid.astype(F32)).astype(BF16)


def _ssd_prepare(problems, a_row):
    hpg = S_HEADS // S_GROUPS
    ii = lax.broadcasted_iota(jnp.int32, (CHUNK, CHUNK), 0)
    jj = lax.broadcasted_iota(jnp.int32, (CHUNK, CHUNK), 1)
    lane = lax.broadcasted_iota(jnp.int32, (CHUNK, LANES), 1)
    tris = {False: jj <= ii, True: jj >= ii}
    tri16 = {r: t.astype(BF16) for r, t in tris.items()}
    splits = [_split3(dt * a_row) for _, dt, _ in problems]
    cums = [sum(_dot(tri16[rev], part) for part in sp) for sp, (_, _, rev) in zip(splits, problems)]
    cum_ts = [c.T for c in cums]
    dt_ts = [dt.T for _, dt, _ in problems]
    bmats = [u[:, S_INNER:S_INNER + LANES] for u, _, _ in problems]
    bmat_ts = [b.T for b in bmats]
    cms = [[jnp.where((lane // S_STATE) == g, u[:, S_INNER + LANES:], 0.0) for g in range(S_GROUPS)]
           for u, _, _ in problems]
    cbs = [[jnp.where(tris[rev], _dot_nt(cm[g].astype(BF16), b.astype(BF16)), 0.0) for g in range(S_GROUPS)]
           for cm, b, (_, _, rev) in zip(cms, bmats, problems)]
    bts = [[jnp.where((ii // S_STATE) == g, bt, 0.0) for g in range(S_GROUPS)] for bt in bmat_ts]
    out = []
    for p, (u, _, rev) in enumerate(problems):
        off = S_HEADS if rev else 0
        tot = cums[p][0:1] if rev else cums[p][CHUNK - 1:CHUNK]
        lhs = []
        wst = []
        decay = []
        for hh in range(S_HEADS):
            g = hh // hpg
            c = off + hh
            col = jnp.broadcast_to(cums[p][:, c:c + 1], (CHUNK, LANES))
            row = cum_ts[p][c:c + 1, :]
            dtr = dt_ts[p][c:c + 1, :]
            dec = jnp.exp(jnp.minimum(col - row, 0.0))
            lhs.append(jnp.concatenate([cbs[p][g] * dec * dtr, cms[p][g] * jnp.exp(col)], axis=1).astype(BF16))
            tot_h = tot[:, c:c + 1]
            wst.append((bts[p][g] * (jnp.exp(tot_h - row) * dtr)).astype(BF16))
            decay.append(jnp.exp(tot_h))
        out.append((u[:, :S_INNER], lhs, wst, decay))
    return out


def _ssd_apply(prep, st):
    xs, lhs, wst, decay = prep
    lane = lax.broadcasted_iota(jnp.int32, (1, LANES), 1)
    lane2 = lax.broadcasted_iota(jnp.int32, (2 * CHUNK, LANES), 1)
    ys = []
    sts = []
    for pair in range(S_HEADS // 2):
        sl = slice(LANES * pair, LANES * (pair + 1))
        both = jnp.concatenate([xs[:, sl], st[:, sl]], axis=0)
        acc = None
        new = None
        for half in range(2):
            hh = 2 * pair + half
            rhs = jnp.where((lane2 // S_HDIM) == half, both, 0.0).astype(BF16)
            part = _dot(lhs[hh], rhs)
            acc = part if acc is None else acc + part
            pn = _dot(wst[hh], rhs[:CHUNK])
            new = pn if new is None else new + pn
        ys.append(acc)
        sts.append(st[:, sl] * jnp.where(lane < S_HDIM, decay[2 * pair], decay[2 * pair + 1]) + new)
    return jnp.concatenate(ys, axis=1), jnp.concatenate(sts, axis=1)


def _ssd_kernel(uf_ref, dtf_ref, ub_ref, dtb_ref, a_ref, initf_ref, initb_ref,
                yf_ref, yb_ref, finf_ref, finb_ref, stf_ref, stb_ref, *, nchunks):
    @pl.when(pl.program_id(1) == 0)
    def _():
        stf_ref[...] = initf_ref[...]
        stb_ref[...] = initb_ref[...]

    nbat = uf_ref.shape[0]
    chains = [(bb, rev) for bb in range(nbat) for rev in (False, True)]
    order = {False: list(range(nchunks)), True: list(reversed(range(nchunks)))}
    problems = []
    for bb, rev in chains:
        u_ref, d_ref = (ub_ref, dtb_ref) if rev else (uf_ref, dtf_ref)
        for k in order[rev]:
            rows = slice(k * CHUNK, (k + 1) * CHUNK)
            problems.append((u_ref[bb, rows, :], d_ref[bb, rows, :], rev))
    preps = _ssd_prepare(problems, a_ref[...])
    states = [(stb_ref if rev else stf_ref)[bb] for bb, rev in chains]
    for step in range(nchunks):
        for c, (bb, rev) in enumerate(chains):
            y, states[c] = _ssd_apply(preps[c * nchunks + step], states[c])
            rows = slice(order[rev][step] * CHUNK, (order[rev][step] + 1) * CHUNK)
            (yb_ref if rev else yf_ref)[bb, rows, :] = y
    for c, (bb, rev) in enumerate(chains):
        (stb_ref if rev else stf_ref)[bb] = states[c]

    @pl.when(pl.program_id(1) == pl.num_programs(1) - 1)
    def _():
        finf_ref[...] = stf_ref[...]
        finb_ref[...] = stb_ref[...]


def _ssd(u, dt, sw, layer, init_f, init_b, nchunks, nbat):
    bsz, ntok, width = u.shape
    blk = nchunks * CHUNK
    nsteps = ntok // blk
    srows = S_GROUPS * S_STATE
    fwd = lambda b, s: (b, s, 0)
    bwd = lambda b, s: (b, nsteps - 1 - s, 0)
    state = pl.BlockSpec((nbat, srows, S_INNER), lambda b, s: (b, 0, 0))
    yshape = jax.ShapeDtypeStruct((bsz, ntok, S_INNER), F32)
    sshape = jax.ShapeDtypeStruct((bsz, srows, S_INNER), F32)
    return pl.pallas_call(
        functools.partial(_ssd_kernel, nchunks=nchunks),
        out_shape=[yshape, yshape, sshape, sshape],
        grid=(bsz // nbat, nsteps),
        in_specs=[pl.BlockSpec((nbat, blk, width), fwd), pl.BlockSpec((nbat, blk, LANES), fwd),
                  pl.BlockSpec((nbat, blk, width), bwd), pl.BlockSpec((nbat, blk, LANES), bwd),
                  _layer(sw["a_row"], layer), state, state],
        out_specs=[pl.BlockSpec((nbat, blk, S_INNER), fwd), pl.BlockSpec((nbat, blk, S_INNER), bwd), state, state],
        scratch_shapes=[pltpu.VMEM((nbat, srows, S_INNER), F32), pltpu.VMEM((nbat, srows, S_INNER), F32)],
        compiler_params=_cparams("parallel", "arbitrary"),
        name="ssd",
    )(u, dt, u, dt, sw["a_row"], init_f, init_b)


def _hyspec_kernel(ft_ref, w1_ref, b1_ref, q1_ref, w2_ref, b2_ref, q2_ref, w3b_ref, w3f_ref, dl_ref, c_ref, s_ref,
                   g_ref, h_ref, f_ref, p0_ref, *, blk, nb):
    n = ft_ref.shape[1]
    half = n // 2
    kt = pl.program_id(0)

    @pl.when(kt == 0)
    def _():
        h = jnp.sin(q1_ref[...] * (_dot(w1_ref[...], ft_ref[:, half:], HIGHEST) + b1_ref[...]))
        hf = jnp.sin(q2_ref[...] * (_dot(w2_ref[...], h, HIGHEST) + b2_ref[...]))
        h_ref[:, half:] = hf
        ii = lax.broadcasted_iota(jnp.int32, (LANES, LANES), 0)
        jj = lax.broadcasted_iota(jnp.int32, (LANES, LANES), 1)
        exchange = (ii + jj == LANES - 1).astype(F32)
        nblk = half // LANES
        rev = jnp.concatenate([_dot(hf[:, (nblk - 1 - b) * LANES:(nblk - b) * LANES], exchange, HIGHEST)
                               for b in range(nblk)], axis=1)
        h_ref[:, :half] = pltpu.roll(rev, 1, 1)
        rt = HY_FILT_ROWS
        for r in range(2 * HY_CH // rt):
            rows = slice(r * rt, (r + 1) * rt)
            fb = _dot(w3b_ref[rows, :], h_ref[:, :half], HIGHEST)
            ff = _dot(w3f_ref[rows, :], h_ref[:, half:], HIGHEST)
            f = jnp.concatenate([fb, ff], axis=1)
            f = f * jnp.exp(-ft_ref[0:1, :] * dl_ref[rows, :])
            pos = lax.broadcasted_iota(jnp.int32, f.shape, 1)
            f = jnp.where(pos == 0, 0.0, f)
            f = f / (jnp.sum(jnp.abs(f), axis=1, keepdims=True) + EPS)
            f_ref[rows, :] = f.astype(BF16)
            for e in range(2 * nb):
                p0_ref[rows, e:e + 1] = f[:, e * blk:e * blk + 1]

    ftile = c_ref.shape[1]
    ctab = c_ref[...]
    stab = s_ref[...]
    freq = lax.broadcasted_iota(jnp.int32, (2 * HY_CH, ftile), 1) + kt * ftile
    sigma = jnp.where((freq & 1) == 0, 1.0, -1.0)
    scale = 2.0 / (2 * blk)
    prev = None
    for e in range(2 * nb):
        p16 = f_ref[:, e * blk:(e + 1) * blk]
        a = _dot(p16, ctab)
        bs = _dot(p16, stab)
        cur = (a, bs, p0_ref[:, e:e + 1])
        if prev is not None:
            gr = (a + sigma * prev[1]) * scale
            gi = (sigma * (prev[0] - prev[2]) - bs) * scale
            for o in range(2):
                g_ref[o, e - 1, 0] = gr[o * HY_CH:(o + 1) * HY_CH]
                g_ref[o, e - 1, 1] = gi[o * HY_CH:(o + 1) * HY_CH]
        prev = cur


def _hyspec(feats_t, sw, layer, ctab, stab, blk):
    n = feats_t.shape[1]
    nb = n // (2 * blk)
    ft = min(HY_SUB, blk)
    rows = 2 * HY_CH
    args = [sw[k] for k in ("w1t", "b1", "q1", "w2t", "b2", "q2", "w3b", "w3f", "delta")]
    return pl.pallas_call(
        functools.partial(_hyspec_kernel, blk=blk, nb=nb),
        out_shape=jax.ShapeDtypeStruct((2, 2 * nb - 1, 2, HY_CH, blk), F32),
        grid=(blk // ft,),
        in_specs=[_full(feats_t)] + [_layer(a, layer) for a in args]
        + [pl.BlockSpec((blk, ft), lambda k: (0, k)), pl.BlockSpec((blk, ft), lambda k: (0, k))],
        out_specs=pl.BlockSpec((2, 2 * nb - 1, 2, HY_CH, ft), lambda k: (0, 0, 0, 0, k)),
        scratch_shapes=[pltpu.VMEM((HY_HIDDEN, n), F32), pltpu.VMEM((rows, n), BF16), pltpu.VMEM((rows, LANES), F32)],
        compiler_params=_cparams("arbitrary"),
        name="hyena_spectra",
    )(feats_t, *args, ctab, stab)


def _hyconv_kernel(u_ref, m_ref, d_ref, g_ref, c_ref, s_ref, o_ref, ub_ref, yr_ref, ys_ref, z_ref, *, blk, nb):
    order = pl.program_id(1)
    kt = pl.program_id(2)
    ft = g_ref.shape[-1]
    cg = HY_CH // HY_CHAINS
    rows_per_chain = nb * cg

    @pl.when(jnp.logical_and(order == 0, kt == 0))
    def _():
        z_ref[...] = u_ref[0, 0]

    @pl.when(kt == 0)
    def _():
        for h in range(HY_CHAINS):
            for j in range(nb):
                r0 = h * rows_per_chain + j * cg
                ub_ref[r0:r0 + cg, :] = z_ref[h * cg:(h + 1) * cg, j * blk:(j + 1) * blk].astype(BF16)

    fs = pl.ds(pl.multiple_of(kt * ft, ft), ft)
    ctab = c_ref[:, fs]
    stab = s_ref[:, fs]
    spectra = []
    for h in range(HY_CHAINS):
        ub = ub_ref[h * rows_per_chain:(h + 1) * rows_per_chain, :]
        spectra.append((_dot(ub, ctab), _dot(ub, stab)))
    for h, (xr, xs) in enumerate(spectra):
        ch = slice(h * cg, (h + 1) * cg)
        for i in range(nb):
            yr = None
            ys = None
            for j in range(nb):
                d = i - j + nb - 1
                gr = g_ref[0, d, 0, ch, :]
                gi = g_ref[0, d, 1, ch, :]
                xrj = xr[j * cg:(j + 1) * cg]
                xsj = xs[j * cg:(j + 1) * cg]
                tr = gr * xrj + gi * xsj
                ts = gr * xsj - gi * xrj
                yr = tr if yr is None else yr + tr
                ys = ts if ys is None else ys + ts
            r0 = h * rows_per_chain + i * cg
            yr_ref[r0:r0 + cg, fs] = yr.astype(BF16)
            ys_ref[r0:r0 + cg, fs] = ys.astype(BF16)

    @pl.when(kt == pl.num_programs(2) - 1)
    def _():
        for h in range(HY_CHAINS):
            ch = slice(h * cg, (h + 1) * cg)
            rows = slice(h * rows_per_chain, (h + 1) * rows_per_chain)
            conv = _dot_nt(yr_ref[rows, :], c_ref[...]) + _dot_nt(ys_ref[rows, :], s_ref[...])
            for i in range(nb):
                cols = slice(i * blk, (i + 1) * blk)
                val = m_ref[0, 0, ch, cols] * (conv[i * cg:(i + 1) * cg] + z_ref[ch, cols] * d_ref[ch, :])
                z_ref[ch, cols] = val
                o_ref[0, ch, cols] = val.astype(o_ref.dtype)


def _hyconv(p4, dcols, layer, gspec, tabs, blk, out_dtype):
    bsz, _, _, n = p4.shape
    nb = n // blk
    ft = min(HY_FT, blk)
    ctab, stab = tabs
    return pl.pallas_call(
        functools.partial(_hyconv_kernel, blk=blk, nb=nb),
        out_shape=jax.ShapeDtypeStruct((bsz, HY_CH, n), out_dtype),
        grid=(bsz, 2, blk // ft),
        in_specs=[pl.BlockSpec((1, 1, HY_CH, n), lambda b, o, k: (b, 0, 0, 0)),
                  pl.BlockSpec((1, 1, HY_CH, n), lambda b, o, k: (b, o + 1, 0, 0)),
                  pl.BlockSpec((None, HY_CH, 1), lambda b, o, k: (2 * layer + o, 0, 0)),
                  pl.BlockSpec((1, 2 * nb - 1, 2, HY_CH, ft), lambda b, o, k: (o, 0, 0, 0, k)),
                  _full(ctab), _full(stab)],
        out_specs=pl.BlockSpec((1, HY_CH, n), lambda b, o, k: (b, 0, 0)),
        scratch_shapes=[pltpu.VMEM((nb * HY_CH, blk), BF16)] * 3 + [pltpu.VMEM((HY_CH, n), F32)],
        compiler_params=_cparams("parallel", "arbitrary", "arbitrary"),
        name="hyena_conv",
    )(p4, p4, dcols, gspec, ctab, stab)


def _dft_tables(blk):
    s = np.arange(blk, dtype=np.int64)[:, None]
    k = np.arange(blk, dtype=np.int64)[None, :]
    ang = ((s * (2 * k + 1)) % (4 * blk)).astype(np.float64) * (2.0 * math.pi / (4 * blk))
    return jnp.asarray(np.cos(ang), F32), jnp.asarray(np.sin(ang), F32)


def _hyena_features(length):
    p = np.arange(length)
    tb = np.where(p == 0, 0, length - p)
    t = np.concatenate([tb, p]).astype(np.float64)
    t01 = t / (length - 1)
    w = (2.0 * math.pi / length) * t
    bands = np.linspace(1e-4, HY_BANDS - 1, HY_BANDS)[:, None]
    feats = np.concatenate([t01[None, :], np.cos(bands * w[None, :]), -np.sin(bands * w[None, :])], axis=0)
    pad = (-feats.shape[0]) % SUBLANES
    return jnp.asarray(np.pad(feats, ((0, pad), (0, 0))), F32)


def _hyena(p4, sw, layer, tabs, blk, out_dtype):
    bsz, _, _, n = p4.shape
    gspec = _hyspec(_hyena_features(n), sw, layer, tabs[0], tabs[1], blk)
    return _hyconv(p4, sw["d_hy"], layer, gspec, tabs, blk, out_dtype)


def _post_kernel(x_ref, att_ref, yf_ref, yb_ref, xs_ref, z_ref, hyo_ref, ga1_ref, sh2_ref, sc2_ref, ga2_ref,
                 dsk_ref, gss_ref, wo_ref, g2_ref, w1_ref, w2_ref, o_ref):
    x = x_ref[0]
    y = yf_ref[0] + yb_ref[0] + xs_ref[0] * dsk_ref[...]
    zz = z_ref[0]
    y = y * (zz * _sigmoid(zz))
    gw = S_INNER // S_GROUPS
    lane = lax.broadcasted_iota(jnp.int32, y.shape, 1)
    first = lane < gw
    y2 = y * y
    s0 = jnp.sum(jnp.where(first, y2, 0.0), axis=-1, keepdims=True)
    s1 = jnp.sum(y2, axis=-1, keepdims=True) - s0
    inv = jnp.where(first, lax.rsqrt(s0 * (1.0 / gw) + EPS), lax.rsqrt(s1 * (1.0 / gw) + EPS))
    ssm = (y * inv * gss_ref[...]).astype(BF16)
    na = N_HEADS * D_V
    mix = _dot(att_ref[0], wo_ref[0:na]) + _dot(ssm, wo_ref[na:na + S_INNER])
    mix = mix + _dot_tn(hyo_ref[0], wo_ref[na + S_INNER:])
    x1 = x + ga1_ref[0] * mix
    h2 = (_rms(x1) * g2_ref[...]) * (1.0 + sc2_ref[0]) + sh2_ref[0]
    hb = h2.astype(BF16)
    acc = jnp.zeros_like(x1)
    fc = FF_CHUNK
    for c in range(D_FF // fc):
        t = jnp.maximum(_dot(hb, w1_ref[:, c * fc:(c + 1) * fc]), 0.0)
        acc = acc + _dot((t * t).astype(BF16), w2_ref[c * fc:(c + 1) * fc, :])
    o_ref[0] = x1 + ga2_ref[0] * acc


def _post(x, att, yf, yb, u, z, hyo, modtok, mod_row0, sw, layer, tm):
    bsz, ntok, _ = x.shape

    def tokspec(width):
        return pl.BlockSpec((1, tm, width), lambda b, i: (b, i, 0))

    weights = [sw[k] for k in ("dskip", "g_ssm", "wo", "g_mlp", "w1", "w2")]
    return pl.pallas_call(
        _post_kernel,
        out_shape=jax.ShapeDtypeStruct((bsz, ntok, D_MODEL), F32),
        grid=(bsz, ntok // tm),
        in_specs=[tokspec(D_MODEL), tokspec(N_HEADS * D_V), tokspec(S_INNER), tokspec(S_INNER), tokspec(S_INNER),
                  tokspec(S_INNER), pl.BlockSpec((1, HY_CH, tm), lambda b, i: (b, 0, i)),
                  _modspec(mod_row0, 2), _modspec(mod_row0, 3), _modspec(mod_row0, 4), _modspec(mod_row0, 5)]
        + [_layer(w, layer) for w in weights],
        out_specs=tokspec(D_MODEL),
        compiler_params=_cparams("parallel", "parallel"),
        name="post",
    )(x, att, yf, yb, u, z, hyo, modtok, modtok, modtok, modtok, *weights)


def _stacked_weights(p):
    depth = p["w_in"].shape[0]
    o = np.cumsum([0, Q_LORA, KV_LORA, D_ROPE, S_INNER, S_XBC, 2 * S_HEADS, 3 * HY_CH])
    w_in_t = jnp.swapaxes(p["w_in"], 1, 2)
    wcq, wckv, wkr, wz, wx, wdt, why = (w_in_t[:, o[j]:o[j + 1]] for j in range(7))

    def partner(t, axis):
        f = ROPE_FREQS
        return jnp.concatenate([lax.slice_in_dim(t, (j ^ 1) * f, ((j ^ 1) + 1) * f, axis=axis)
                                for j in range(D_ROPE // f)], axis=axis)

    def pad_rows(t, lo, hi):
        return jnp.pad(t, ((0, 0), (lo, hi), (0, 0)))

    wall = jnp.concatenate([wcq, wckv, pad_rows(wkr, D_NOPE, HEAD_PAD - D_QK),
                            pad_rows(partner(wkr, 1), D_NOPE, HEAD_PAD - D_QK), wz, wx, why,
                            pad_rows(wdt, 0, LANES - 2 * S_HEADS)], axis=1).astype(BF16)
    assert wall.shape[1] == W_ALL
    hw = N_HEADS * HEAD_PAD
    headpad = lambda t, lo, hi: jnp.pad(t, ((0, 0), (0, 0), (0, 0), (lo, hi))).reshape(depth, t.shape[1], hw)
    wuq4 = p["w_uq"].reshape(depth, Q_LORA, N_HEADS, D_QK)
    wukv = p["w_ukv"].reshape(depth, KV_LORA, N_HEADS, D_NOPE + D_V)
    odd = (np.arange(N_HEADS) % 2 == 1)[None, None, :, None]
    wv4 = jnp.where(odd, jnp.pad(wukv[..., D_NOPE:], ((0, 0), (0, 0), (0, 0), (HEAD_PAD - D_V, 0))),
                    jnp.pad(wukv[..., D_NOPE:], ((0, 0), (0, 0), (0, 0), (0, HEAD_PAD - D_V))))
    vone = np.zeros((N_HEADS, HEAD_PAD), np.float32)
    vone[0::2, D_V] = 1.0
    vone[1::2, 0] = 1.0
    row = lambda t: t.reshape(depth, 1, -1)
    lanepad = lambda t, lo, hi: jnp.pad(t, ((0, 0), (0, 0), (lo, hi)))
    taps = lambda w, b: jnp.pad(jnp.concatenate([w, b[:, None, :]], axis=1), ((0, 0), (0, SUBLANES - 4), (0, 0)))
    col = lambda t: t.reshape(depth, -1, 1)
    qscale = math.log2(math.e) / math.sqrt(D_QK)
    delta = np.abs(np.linspace(math.log(HY_DECAY_TARGET) / HY_DECAY_PCT_LONG,
                               math.log(HY_DECAY_TARGET) / HY_DECAY_PCT_SHORT, HY_CH))
    return dict(
        g_mix=row(p["g_norm_mix"]), wall=wall,
        taps_x=taps(p["w_conv_ssm"], p["b_conv_ssm"]), taps_hy=taps(p["w_conv_hy"], p["b_conv_hy"]),
        dt_b=lanepad(row(p["dt_bias"]), 0, LANES - 2 * S_HEADS),
        g_cq=row(p["g_cq"]), g_ckv=row(p["g_ckv"]),
        wuq=headpad(wuq4, 0, HEAD_PAD - D_QK).astype(BF16),
        wuqp=headpad(partner(wuq4[..., D_NOPE:], 3), D_NOPE, HEAD_PAD - D_QK).astype(BF16),
        wk=headpad(wukv[..., :D_NOPE], 0, HEAD_PAD - D_NOPE).astype(BF16),
        wv=wv4.reshape(depth, KV_LORA, hw).astype(BF16),
        vone=jnp.asarray(np.tile(vone.reshape(1, 1, hw), (depth, 1, 1))),
        g_q=lanepad(row(p["g_qhead"]), 0, HEAD_PAD - D_QK) * qscale, g_k=lanepad(row(p["g_khead"]), 0, HEAD_PAD - D_QK),
        g_qp=lanepad(partner(row(p["g_qhead"])[..., D_NOPE:], 2), D_NOPE, HEAD_PAD - D_QK) * qscale,
        g_kp=lanepad(partner(row(p["g_khead"])[..., D_NOPE:], 2), D_NOPE, HEAD_PAD - D_QK),
        a_row=lanepad(row(-jnp.exp(p["a_log"].astype(F32))), 0, LANES - 2 * S_HEADS),
        dskip=row(jnp.repeat(p["d_skip_ssm"], S_HDIM, axis=1)), g_ssm=row(p["g_ssm_out"]),
        wo=p["w_out"].astype(BF16), g_mlp=row(p["g_norm_mlp"]),
        w1=p["w_ff1"].astype(BF16), w2=p["w_ff2"].astype(BF16),
        w1t=lanepad(jnp.swapaxes(p["w_f1"], 1, 2), 0, (-HY_EMB) % SUBLANES),
        b1=col(p["b_f1"]), q1=col(p["freq_f1"]), w2t=jnp.swapaxes(p["w_f2"], 1, 2),
        b2=col(p["b_f2"]), q2=col(p["freq_f2"]),
        w3f=jnp.swapaxes(p["w_f3"][:, :, :2 * HY_CH], 1, 2), w3b=jnp.swapaxes(p["w_f3"][:, :, 2 * HY_CH:], 1, 2),
        delta=jnp.asarray(np.tile(delta, (depth, 2)).reshape(depth, 2 * HY_CH, 1), F32),
        d_hy=p["d_skip_hy"].reshape(depth * 2, HY_CH, 1),
    )


def _rope_tables(seq):
    pos = np.arange(seq)
    inv = ROPE_THETA ** (-np.arange(ROPE_FREQS, dtype=np.float64) / ROPE_FREQS)
    ang = np.stack([(pos // GRID_W)[:, None] * inv, (pos % GRID_W)[:, None] * inv], axis=1)
    cos, sin = np.cos(ang), np.sin(ang)
    ct = np.ones((seq, HEAD_PAD))
    sn = np.zeros((seq, HEAD_PAD))
    for axis in range(2):
        lo = D_NOPE + axis * 2 * ROPE_FREQS
        mid = lo + ROPE_FREQS
        ct[:, lo:mid] = cos[:, axis]
        ct[:, mid:mid + ROPE_FREQS] = cos[:, axis]
        sn[:, lo:mid] = -sin[:, axis]
        sn[:, mid:mid + ROPE_FREQS] = sin[:, axis]
    return jnp.asarray(ct, F32), jnp.asarray(sn, F32)


def kernel(x, c, ctx, c_ctx, w_mod, b_mod, g_norm_mix, g_norm_mlp, w_in, w_out, g_cq, g_ckv, w_uq, w_ukv, g_qhead, g_khead, w_conv_ssm, b_conv_ssm, a_log, dt_bias, d_skip_ssm, g_ssm_out, w_conv_hy, b_conv_hy, w_f1, b_f1, freq_f1, w_f2, b_f2, freq_f2, w_f3, d_skip_hy, w_ff1, w_ff2):
    params = dict(w_in=w_in, w_out=w_out, g_norm_mix=g_norm_mix, g_norm_mlp=g_norm_mlp, g_cq=g_cq, g_ckv=g_ckv,
                  w_uq=w_uq, w_ukv=w_ukv, g_qhead=g_qhead, g_khead=g_khead, a_log=a_log, d_skip_ssm=d_skip_ssm,
                  g_ssm_out=g_ssm_out, w_conv_hy=w_conv_hy, b_conv_hy=b_conv_hy, w_f1=w_f1, b_f1=b_f1,
                  freq_f1=freq_f1, w_f2=w_f2, b_f2=b_f2, freq_f2=freq_f2, w_f3=w_f3, d_skip_hy=d_skip_hy,
                  w_ff1=w_ff1, w_ff2=w_ff2, w_conv_ssm=w_conv_ssm, b_conv_ssm=b_conv_ssm, dt_bias=dt_bias)
    bsz, seq, _ = x.shape
    nctx = ctx.shape[1]
    assert seq % HY_BLOCK == 0 and seq % TM == 0 and (bsz * nctx) % TM == 0 and TM % nctx == 0
    assert bsz + 1 <= SUBLANES and CHUNK == S_GROUPS * S_STATE and nctx % CHUNK == 0

    cvec = jnp.pad(jnp.concatenate([c, c_ctx[None, :]], axis=0), ((0, SUBLANES - bsz - 1), (0, 0)))
    mod = _modulation(cvec, w_mod, b_mod)
    modtok = mod.reshape(DEPTH * SUBLANES, 1, 6 * D_MODEL)

    sw = _stacked_weights(params)
    rope_tabs = _rope_tables(seq)
    tabs_lat = tuple(t.astype(BF16) for t in _dft_tables(HY_BLOCK))
    tabs_ctx = tuple(t.astype(BF16) for t in _dft_tables(nctx))
    zero_state = jnp.zeros((bsz, S_GROUPS * S_STATE, S_INNER), F32)

    xl = x
    xc = ctx.reshape(1, bsz * nctx, D_MODEL)
    for i in range(DEPTH):
        last = i == DEPTH - 1
        row_l = i * SUBLANES
        row_c = i * SUBLANES + bsz

        q_c, k_c, v_c, z_c, u_c, dt_c, hyt_c = _inproj(xc, modtok, row_c, sw, i, None, TM, nctx)
        per_b = lambda t: t.reshape(bsz, nctx, t.shape[-1])
        k_c, v_c, u_c, dt_c = per_b(k_c), per_b(v_c), per_b(u_c), per_b(dt_c)
        yf_c, yb_c, s_fwd, s_bwd = _ssd(u_c, dt_c, sw, i, zero_state, zero_state, nctx // CHUNK, SSD_BATCH)

        q, k, v, z, u, dt, hyt = _inproj(xl, modtok, row_l, sw, i, rope_tabs, TM, seq)
        att = _attention(q, [(k_c, v_c), (k, v)], TQ)
        yf, yb, _, _ = _ssd(u, dt, sw, i, s_fwd, s_bwd, SSD_GROUP, SSD_BATCH)
        hyo = _hyena(hyt, sw, i, tabs_lat, HY_BLOCK, BF16)
        xl = _post(xl, att, yf, yb, u, z, hyo, modtok, row_l, sw, i, TM)
        if last:
            return xl

        att_c = _attention(per_b(q_c), [(k_c, v_c)], nctx)
        hyt_cb = hyt_c.reshape(3, HY_CH, bsz, nctx).transpose(2, 0, 1, 3)
        hyo_c = _hyena(hyt_cb, sw, i, tabs_ctx, nctx, BF16)
        flat = lambda t: t.reshape(1, bsz * nctx, t.shape[-1])
        hyo_cf = hyo_c.transpose(1, 0, 2).reshape(1, HY_CH, bsz * nctx)
        xc = _post(xc, flat(att_c), flat(yf_c), flat(yb_c), flat(u_c), z_c, hyo_cf, modtok, row_c, sw, i, TM)
```
